```python
import math
import jax, jax.numpy as jnp
from jax import lax
import numpy as np

D_MODEL = 1024
BATCH = 32
SEQ = 2048
DEPTH = 1

GRID_W = 64
CTX_LEN = 256
MIX_W = D_MODEL
S5_W = MIX_W // 2
S5_H = 16
S5_G = S5_W // S5_H
S5_P = 64
HY_W = MIX_W - S5_W
HY_ORDER = 2
HY_SHORT = 3
HF_BANDS = 16
HF_EMB = 2 * HF_BANDS + 1
HF_HID = 64
HF_INNER = 2
HF_DECAY_SLOW = -math.log(1e-2) / 1.5
HF_DECAY_FAST = -math.log(1e-2) / 0.3
PROJ_W = S5_W + (HY_ORDER + 1) * HY_W
PEER_HEADS = 8
PEER_NKEYS = 128
PEER_EXPERTS = PEER_NKEYS * PEER_NKEYS
PEER_DK = 256
PEER_DK_HALF = PEER_DK // 2
PEER_TOPK = 16
PEER_CHUNK = 128
N_MOD = 6
EPS = 1e-6

kernel_name = "hybrid_s5_hyena_peer_dit_block"


def rmsnorm(x, g):
    xf = x.astype(jnp.float32)
    y = xf * lax.rsqrt(jnp.mean(xf * xf, axis=-1, keepdims=True) + EPS) * g.astype(jnp.float32)
    return y.astype(x.dtype)


def modulate(h, shift, scale):
    return h * (1 + scale) + shift


def short_conv(x, w, b):
    pad = HY_SHORT // 2
    W = x.shape[1]
    xp = jnp.pad(x, ((0, 0), (pad, pad), (0, 0)))
    y = b
    for j in range(HY_SHORT):
        y = y + w[j] * xp[:, j:j + W]
    return y


def s5_discretise(a_re, a_im, log_step, b_re, b_im):
    lam = lax.complex(a_re.astype(jnp.float32), a_im.astype(jnp.float32))
    step = jnp.exp(log_step.astype(jnp.float32))[:, None]
    lam_bar = jnp.exp(lam * step)
    b = lax.complex(b_re.astype(jnp.float32), b_im.astype(jnp.float32))
    b_bar = ((lam_bar - 1.0) / lam)[..., None] * b
    return lam_bar, b_bar


def _linear_combine(e1, e2):
    a1, b1 = e1
    a2, b2 = e2
    return a1 * a2, a2 * b1 + b2


def s5_states(u, lam_bar, b_bar, h0, reverse):
    bu = jnp.einsum('nlgh,gph->nlgp', u.astype(jnp.float32).astype(jnp.complex64), b_bar)
    if reverse:
        bu = jnp.flip(bu, axis=1)
    if h0 is not None:
        bu = bu.at[:, 0].add(lam_bar * h0)
    a = jnp.broadcast_to(lam_bar, (1, bu.shape[1]) + lam_bar.shape)
    _, st = lax.associative_scan(_linear_combine, (a, bu), axis=1)
    if reverse:
        st = jnp.flip(st, axis=1)
    return st


def s5_readout(st, c_re, c_im):
    cc = lax.complex(c_re.astype(jnp.float32), c_im.astype(jnp.float32))
    return jnp.real(jnp.einsum('nlgp,ghp->nlgh', st, cc))


def s5_glu(y, w, b):
    gy = jax.nn.gelu(y)
    return gy * jax.nn.sigmoid(gy @ w.astype(jnp.float32) + b.astype(jnp.float32))


def hyena_filters(L, w1, b1, wh, bh, freq, wout, decay):
    f32 = jnp.float32
    pos = jnp.arange(L, dtype=f32)
    t = pos / max(L - 1, 1)
    bands = jnp.linspace(1e-4, HF_BANDS - 1, HF_BANDS, dtype=f32)
    ang = (2.0 * math.pi / L) * pos[:, None] * bands[None, :]
    feats = jnp.concatenate([t[:, None], jnp.cos(ang), -jnp.sin(ang)], axis=-1)
    fr = freq.astype(f32)
    hid = jnp.sin(fr * (feats @ w1.astype(f32) + b1.astype(f32)))
    for i in range(HF_INNER):
        hid = jnp.sin(fr * (hid @ wh[i].astype(f32) + bh[i].astype(f32)))
    h = (hid @ wout.astype(f32)) * jnp.exp(-t[:, None] * jnp.abs(decay.astype(f32)))
    h = h.reshape(L, HY_ORDER, 2, HY_W)
    return h * lax.rsqrt(jnp.sum(h * h, axis=(0, 2), keepdims=True) + EPS)


def bidir_long_conv(z, h_fwd, h_bwd):
    L, C = h_fwd.shape
    k = jnp.concatenate([h_fwd, jnp.zeros((1, C), jnp.float32), h_bwd[:0:-1]], axis=0)
    zf = jnp.fft.rfft(z.astype(jnp.float32), n=2 * L, axis=1)
    kf = jnp.fft.rfft(k, axis=0)
    return jnp.fft.irfft(zf * kf[None], n=2 * L, axis=1)[:, :L]


def hyena_mixer(p, conv_w, conv_b, filt, d_hy, n_rows):
    N, L, C = p.shape
    q = short_conv(p.reshape(N * n_rows, L // n_rows, C), conv_w, conv_b)
    q = q.reshape(N, L, C).astype(jnp.float32)
    v, x1, x2 = jnp.split(q, HY_ORDER + 1, axis=-1)
    z = v
    for o, gate in enumerate((x1, x2)):
        z = gate * (bidir_long_conv(z, filt[:, o, 0], filt[:, o, 1]) + d_hy[o].astype(jnp.float32) * z)
    return z


def merge_head_groups(y_s5, z_hy, w_glu, b_glu, g_s5, g_hy, w_out, dtype):
    s5_out = s5_glu(y_s5, w_glu, b_glu)
    cat = jnp.concatenate([rmsnorm(s5_out, g_s5), rmsnorm(z_hy, g_hy)], axis=-1)
    return cat.astype(dtype) @ w_out


def peer(h, wq, k1, k2, u_tab, v_tab):
    T, D = h.shape
    f32 = jnp.float32
    k1f = k1.astype(f32)
    k2f = k2.astype(f32)

    def block(hb):
        q = (hb @ wq).astype(f32).reshape(hb.shape[0], PEER_HEADS, 2, PEER_DK_HALF)
        s1 = jnp.einsum('chd,hnd->chn', q[:, :, 0], k1f)
        s2 = jnp.einsum('chd,hnd->chn', q[:, :, 1], k2f)
        v1, i1 = lax.top_k(s1, PEER_TOPK)
        v2, i2 = lax.top_k(s2, PEER_TOPK)
        cand = (v1[..., :, None] + v2[..., None, :]).reshape(v1.shape[:-1] + (PEER_TOPK * PEER_TOPK,))
        sc, ci = lax.top_k(cand, PEER_TOPK)
        e = (jnp.take_along_axis(i1, ci // PEER_TOPK, axis=-1) * PEER_NKEYS
             + jnp.take_along_axis(i2, ci % PEER_TOPK, axis=-1))
        g = jax.nn.softmax(sc, axis=-1)
        act = jax.nn.gelu(jnp.einsum('chkd,cd->chk', jnp.take(u_tab, e, axis=0), hb).astype(f32))
        return jnp.einsum('chk,chkd->cd', (g * act).astype(hb.dtype), jnp.take(v_tab, e, axis=0))

    out = lax.map(block, h.reshape(T // PEER_CHUNK, PEER_CHUNK, D))
    return out.reshape(T, D)


def setup_inputs(seed: int = 0) -> dict:
    key = jax.random.key(seed)
    ks = iter(jax.random.split(key, 64))
    f32 = jnp.float32

    def nrm(shape, s):
        return s * jax.random.normal(next(ks), shape, f32)

    n_idx = jnp.arange(S5_P, dtype=f32)
    return {
        "x": nrm((BATCH, SEQ, D_MODEL), 1.0),
        "c": nrm((BATCH, D_MODEL), 1.0),
        "ctx": nrm((BATCH, CTX_LEN, D_MODEL), 1.0),
        "c_ctx": nrm((D_MODEL,), 1.0),
        "w_ada": nrm((DEPTH, D_MODEL, N_MOD * D_MODEL), 0.5 * D_MODEL ** -0.5),
        "b_ada": nrm((DEPTH, N_MOD * D_MODEL), 0.02),
        "g_norm1": 1.0 + nrm((DEPTH, D_MODEL), 0.02),
        "g_norm2": 1.0 + nrm((DEPTH, D_MODEL), 0.02),
        "w_in": nrm((DEPTH, D_MODEL, PROJ_W), D_MODEL ** -0.5),
        "b_in": nrm((DEPTH, PROJ_W), 0.02),
        "s5_a_re": -0.5 + nrm((DEPTH, 2, S5_G, S5_P), 0.01),
        "s5_a_im": math.pi * n_idx + nrm((DEPTH, 2, S5_G, S5_P), 0.01),
        "s5_log_step": jax.random.uniform(next(ks), (DEPTH, 2, S5_G), f32, math.log(1e-3), math.log(1e-1)),
        "s5_b_re": nrm((DEPTH, 2, S5_G, S5_P, S5_H), (2 * S5_H) ** -0.5),
        "s5_b_im": nrm((DEPTH, 2, S5_G, S5_P, S5_H), (2 * S5_H) ** -0.5),
        "s5_c_re": nrm((DEPTH, 2, S5_G, S5_H, S5_P), 0.5),
        "s5_c_im": nrm((DEPTH, 2, S5_G, S5_H, S5_P), 0.5),
        "s5_d": nrm((DEPTH, S5_W), 1.0),
        "w_glu": nrm((DEPTH, S5_W, S5_W), S5_W ** -0.5),
        "b_glu": nrm((DEPTH, S5_W), 0.02),
        "hy_conv_w": nrm((DEPTH, HY_SHORT, (HY_ORDER + 1) * HY_W), HY_SHORT ** -0.5),
        "hy_conv_b": nrm((DEPTH, (HY_ORDER + 1) * HY_W), 0.02),
        "hf_w1": nrm((DEPTH, HF_EMB, HF_HID), HF_EMB ** -0.5),
        "hf_b1": nrm((DEPTH, HF_HID), 0.02),
        "hf_wh": nrm((DEPTH, HF_INNER, HF_HID, HF_HID), HF_HID ** -0.5),
        "hf_bh": nrm((DEPTH, HF_INNER, HF_HID), 0.02),
        "hf_freq": 1.0 + nrm((DEPTH, HF_HID), 0.01),
        "hf_wout": nrm((DEPTH, HF_HID, HY_ORDER * 2 * HY_W), HF_HID ** -0.5),
        "hf_decay": jnp.tile(jnp.linspace(HF_DECAY_SLOW, HF_DECAY_FAST, HY_W, dtype=f32), HY_ORDER * 2)[None]
                    + nrm((DEPTH, HY_ORDER * 2 * HY_W), 0.1),
        "hy_d": nrm((DEPTH, HY_ORDER, HY_W), 0.1),
        "g_out_s5": 1.0 + nrm((DEPTH, S5_W), 0.02),
        "g_out_hy": 1.0 + nrm((DEPTH, HY_W), 0.02),
        "w_out": nrm((DEPTH, MIX_W, D_MODEL), MIX_W ** -0.5),
        "peer_wq": nrm((DEPTH, D_MODEL, PEER_HEADS * PEER_DK), D_MODEL ** -0.5),
        "peer_k1": nrm((DEPTH, PEER_HEADS, PEER_NKEYS, PEER_DK_HALF), PEER_DK_HALF ** -0.5),
        "peer_k2": nrm((DEPTH, PEER_HEADS, PEER_NKEYS, PEER_DK_HALF), PEER_DK_HALF ** -0.5),
        "peer_u": nrm((DEPTH, PEER_EXPERTS, D_MODEL), D_MODEL ** -0.5),
        "peer_v": nrm((DEPTH, PEER_EXPERTS, D_MODEL), 0.5),
        "g_final": 1.0 + nrm((D_MODEL,), 0.02),
    }


def reference(x, c, ctx, c_ctx, w_ada, b_ada, g_norm1, g_norm2, w_in, b_in,
              s5_a_re, s5_a_im, s5_log_step, s5_b_re, s5_b_im, s5_c_re, s5_c_im, s5_d, w_glu, b_glu,
              hy_conv_w, hy_conv_b, hf_w1, hf_b1, hf_wh, hf_bh, hf_freq, hf_wout, hf_decay, hy_d,
              g_out_s5, g_out_hy, w_out, peer_wq, peer_k1, peer_k2, peer_u, peer_v, g_final):
    B, L, D = x.shape
    Lc = ctx.shape[1]
    rows = L // GRID_W
    dt = x.dtype
    for l in range(DEPTH):
        last = l == DEPTH - 1
        mod_x = (jax.nn.silu(c) @ w_ada[l] + b_ada[l])[:, None, :]
        mod_c = (jax.nn.silu(c_ctx) @ w_ada[l] + b_ada[l])[None, None, :]
        shx1, scx1, gx1, shx2, scx2, gx2 = jnp.split(mod_x, N_MOD, axis=-1)
        shc1, scc1, gc1, shc2, scc2, gc2 = jnp.split(mod_c, N_MOD, axis=-1)

        px = modulate(rmsnorm(x, g_norm1[l]), shx1, scx1) @ w_in[l] + b_in[l]
        pc = modulate(rmsnorm(ctx, g_norm1[l]), shc1, scc1) @ w_in[l] + b_in[l]

        d5 = s5_d[l].astype(jnp.float32).reshape(S5_G, S5_H)
        ux = px[..., :S5_W].astype(jnp.float32).reshape(B, L, S5_G, S5_H)
        uc = pc[..., :S5_W].astype(jnp.float32).reshape(B, Lc, S5_G, S5_H)
        y_x = d5 * ux
        y_c = None if last else d5 * uc
        for dr in range(2):
            lam_bar, b_bar = s5_discretise(s5_a_re[l, dr], s5_a_im[l, dr], s5_log_step[l, dr],
                                           s5_b_re[l, dr], s5_b_im[l, dr])
            st_c = s5_states(uc, lam_bar, b_bar, None, dr == 1)
            h0 = st_c[:, -1] if dr == 0 else st_c[:, 0]
            st_x = s5_states(ux, lam_bar, b_bar, h0, dr == 1)
            y_x = y_x + s5_readout(st_x, s5_c_re[l, dr], s5_c_im[l, dr])
            if y_c is not None:
                y_c = y_c + s5_readout(st_c, s5_c_re[l, dr], s5_c_im[l, dr])

        filt_x = hyena_filters(L, hf_w1[l], hf_b1[l], hf_wh[l], hf_bh[l], hf_freq[l], hf_wout[l], hf_decay[l])
        z_x = hyena_mixer(px[..., S5_W:], hy_conv_w[l], hy_conv_b[l], filt_x, hy_d[l], rows)
        mix_x = merge_head_groups(y_x.reshape(B, L, S5_W), z_x, w_glu[l], b_glu[l],
                                  g_out_s5[l], g_out_hy[l], w_out[l], dt)

        if not last:
            filt_c = hyena_filters(Lc, hf_w1[l], hf_b1[l], hf_wh[l], hf_bh[l], hf_freq[l], hf_wout[l], hf_decay[l])
            z_c = hyena_mixer(pc[..., S5_W:], hy_conv_w[l], hy_conv_b[l], filt_c, hy_d[l], 1)
            mix_c = merge_head_groups(y_c.reshape(B, Lc, S5_W), z_c, w_glu[l], b_glu[l],
                                      g_out_s5[l], g_out_hy[l], w_out[l], dt)
            ctx = ctx + gc1 * mix_c
            hc2 = modulate(rmsnorm(ctx, g_norm2[l]), shc2, scc2)
            ctx = ctx + gc2 * peer(hc2.reshape(B * Lc, D), peer_wq[l], peer_k1[l], peer_k2[l],
                                   peer_u[l], peer_v[l]).reshape(B, Lc, D)

        x = x + gx1 * mix_x
        hx2 = modulate(rmsnorm(x, g_norm2[l]), shx2, scx2)
        x = x + gx2 * peer(hx2.reshape(B * L, D), peer_wq[l], peer_k1[l], peer_k2[l],
                           peer_u[l], peer_v[l]).reshape(B, L, D)
    return rmsnorm(x, g_final)
```

```python
import functools
import math

import jax
import jax.numpy as jnp
from jax import lax
from jax.experimental import pallas as pl
from jax.experimental.pallas import tpu as pltpu

F32 = jnp.float32
BF16 = jnp.bfloat16
HIGHEST = lax.Precision.HIGHEST

EPS = 1e-6
GRID_W = 64
PEER_TOPK = 16
N_MOD = 6

MXU_W = 256
VMEM_LIMIT = 56 * 1024 * 1024


def _cparams(sem):
    return pltpu.CompilerParams(dimension_semantics=sem, vmem_limit_bytes=VMEM_LIMIT)


def _gelu(x):
    c = math.sqrt(2.0 / math.pi)
    return 0.5 * x * (1.0 + jnp.tanh(c * (x + 0.044715 * (x * x * x))))


def _sigmoid(x):
    return 1.0 / (1.0 + jnp.exp(-x))


def _rms(x, g):
    return x * lax.rsqrt(jnp.mean(x * x, axis=-1, keepdims=True) + EPS) * g


def _ada_kernel(c_ref, w_ref, b_ref, o_ref):
    c = c_ref[...]
    s = c * _sigmoid(c)
    o_ref[...] = jnp.dot(s, w_ref[...], preferred_element_type=F32, precision=HIGHEST) + b_ref[...]


def _ada(cc, w, b):
    R, D = cc.shape
    N = w.shape[1]
    tn = 1024
    return pl.pallas_call(
        _ada_kernel,
        grid=(N // tn,),
        in_specs=[pl.BlockSpec((R, D), lambda j: (0, 0)),
                  pl.BlockSpec((D, tn), lambda j: (0, j)),
                  pl.BlockSpec((1, tn), lambda j: (0, j))],
        out_specs=pl.BlockSpec((R, tn), lambda j: (0, j)),
        out_shape=jax.ShapeDtypeStruct((R, N), F32),
        compiler_params=_cparams(("arbitrary",)),
        name="ada_mod",
    )(cc, w, b.reshape(1, N))


def _inproj_kernel(x_ref, mod_ref, g_ref, w_ref, b_ref, *rest, s5_w, hy_w, hyena):
    x = x_ref[0]
    h = _rms(x, g_ref[...])
    h = h * (1.0 + mod_ref[0, 1:2, :]) + mod_ref[0, 0:1, :]
    p = jnp.dot(h.astype(BF16), w_ref[...], preferred_element_type=F32) + b_ref[...]
    if not hyena:
        (u_ref,) = rest
        u_ref[...] = p
        return
    cw_ref, cb_ref, u_ref, v_ref, x1_ref, x2_ref = rest
    u_ref[...] = p[:, :s5_w]
    ph = p[:, s5_w:]
    tl = ph.shape[0]
    pos = lax.broadcasted_iota(jnp.int32, ph.shape, 0) % GRID_W
    prev = jnp.where(pos == 0, 0.0, pltpu.roll(ph, 1, axis=0))
    nxt = jnp.where(pos == GRID_W - 1, 0.0, pltpu.roll(ph, tl - 1, axis=0))
    q = cb_ref[...] + cw_ref[0:1, :] * prev + cw_ref[1:2, :] * ph + cw_ref[2:3, :] * nxt
    v_ref[...] = q[:, :hy_w]
    x1_ref[...] = q[:, hy_w:2 * hy_w]
    x2_ref[...] = q[:, 2 * hy_w:]


def _inproj(x, mod3, g, w_bf, b, conv_w, conv_b, *, s5_w, hy_w, hyena):
    B, L, D = x.shape
    N = w_bf.shape[1]
    tl = min(512, L)
    nb = mod3.shape[0]
    mod_map = (lambda b_, t: (b_, 0, 0)) if nb > 1 else (lambda b_, t: (0, 0, 0))
    in_specs = [pl.BlockSpec((1, tl, D), lambda b_, t: (b_, t, 0)),
                pl.BlockSpec((1, N_MOD, D), mod_map),
                pl.BlockSpec((1, D), lambda b_, t: (0, 0)),
                pl.BlockSpec((D, N), lambda b_, t: (0, 0)),
                pl.BlockSpec((1, N), lambda b_, t: (0, 0))]
    args = [x, mod3, g.reshape(1, D), w_bf, b.reshape(1, N)]
    out_specs = [pl.BlockSpec((tl, s5_w), lambda b_, t: (t, b_))]
    out_shape = [jax.ShapeDtypeStruct((L, B * s5_w), F32)]
    if hyena:
        nh = N - s5_w
        in_specs += [pl.BlockSpec((3, nh), lambda b_, t: (0, 0)),
                     pl.BlockSpec((1, nh), lambda b_, t: (0, 0))]
        args += [conv_w, conv_b.reshape(1, nh)]
        out_specs += [pl.BlockSpec((tl, hy_w), lambda b_, t: (t, b_))] * 3
        out_shape += [jax.ShapeDtypeStruct((L, B * hy_w), F32)] * 3
    return pl.pallas_call(
        functools.partial(_inproj_kernel, s5_w=s5_w, hy_w=hy_w, hyena=hyena),
        grid=(B, L // tl),
        in_specs=in_specs, out_specs=out_specs, out_shape=out_shape,
        compiler_params=_cparams(("arbitrary", "arbitrary")),
        name="inproj_hy" if hyena else "inproj_ctx",
    )(*args)


def _s5_prep_kernel(are_ref, aim_ref, ls_ref, bre_ref, bim_ref, lr_ref, li_ref, bbr_ref, bbi_ref):
    a_re, a_im = are_ref[...], aim_ref[...]
    step = jnp.exp(ls_ref[...])
    er = jnp.exp(a_re * step)
    lr = er * jnp.cos(a_im * step)
    li = er * jnp.sin(a_im * step)
    nr, ni = lr - 1.0, li
    d2 = a_re * a_re + a_im * a_im
    kr = (nr * a_re + ni * a_im) / d2
    ki = (ni * a_re - nr * a_im) / d2
    br, bi = bre_ref[...], bim_ref[...]
    lr_ref[...] = lr
    li_ref[...] = li
    bbr_ref[...] = kr * br - ki * bi
    bbi_ref[...] = kr * bi + ki * br


def _s5_prep(a_re, a_im, log_step, b_re, b_im):
    two, G, P, H = b_re.shape
    n = two * G * P
    col = lambda a: a.reshape(n, 1)
    ls = jnp.broadcast_to(log_step[:, :, None], (two, G, P))
    outs = pl.pallas_call(
        _s5_prep_kernel,
        out_shape=[jax.ShapeDtypeStruct((n, 1), F32)] * 2 + [jax.ShapeDtypeStruct((n, H), F32)] * 2,
        name="s5_prep",
    )(col(a_re), col(a_im), col(ls), b_re.reshape(n, H), b_im.reshape(n, H))
    lr, li, bbr, bbi = outs
    return (lr.reshape(two, G, P), li.reshape(two, G, P),
            bbr.reshape(two, G, P, H), bbi.reshape(two, G, P, H))


def _block_diag(m, gb):
    two, G, R, C = m.shape
    m = m.reshape(two, G // gb, gb, R, C)
    eye = jnp.eye(gb, dtype=m.dtype)
    out = m[:, :, :, :, None, :] * eye[None, None, :, None, :, None]
    return out.reshape(two, G // gb, gb * R, gb * C)


def _s5_kernel(u_ref, h0_ref, lam_ref, wb_ref, cm_ref, d_ref, *rest, tl, nb, cw, emit_y):
    if emit_y:
        y_ref, hfin_ref, sre, sim, hre, him = rest
    else:
        hfin_ref, sre, sim, hre, him = rest
    dr = pl.program_id(0)
    i = pl.program_id(1)
    B = u_ref.shape[1]
    SW = sre.shape[1]
    nsb = wb_ref.shape[2]
    kin = wb_ref.shape[3]
    kst = wb_ref.shape[4]

    @pl.when(i == 0)
    def _():
        hre[...] = h0_ref[0, 0]
        him[...] = h0_ref[0, 1]

    u = u_ref[...].reshape(tl * B, u_ref.shape[2])
    ub = u.astype(BF16)
    for j in range(nsb):
        uj = ub[:, j * kin:(j + 1) * kin]
        sre[:, j * kst:(j + 1) * kst] = jnp.dot(uj, wb_ref[0, 0, j], preferred_element_type=F32)
        sim[:, j * kst:(j + 1) * kst] = jnp.dot(uj, wb_ref[0, 1, j], preferred_element_type=F32)

    for c in range(SW // cw):
        cols = slice(c * cw, (c + 1) * cw)
        lr = jnp.broadcast_to(lam_ref[0, 0:1, cols], (B, cw))
        li = jnp.broadcast_to(lam_ref[0, 1:2, cols], (B, cw))

        def body(s, carry, cols=cols, lr=lr, li=li):
            h_r, h_i = carry
            t = jnp.where(dr == 0, s, tl - 1 - s)
            row = pl.multiple_of(t * B, B)
            n_r = lr * h_r - li * h_i + sre[pl.ds(row, B), cols]
            n_i = lr * h_i + li * h_r + sim[pl.ds(row, B), cols]
            sre[pl.ds(row, B), cols] = n_r
            sim[pl.ds(row, B), cols] = n_i
            return n_r, n_i

        h_r, h_i = lax.fori_loop(0, tl, body, (hre[:, cols], him[:, cols]), unroll=2)
        hre[:, cols] = h_r
        him[:, cols] = h_i

    hfin_ref[0, 0] = hre[...]
    hfin_ref[0, 1] = him[...]

    if emit_y:
        kout = cm_ref.shape[4]
        dsel = d_ref[...] * jnp.where(dr == 0, 1.0, 0.0)
        for n in range(nsb):
            y = (jnp.dot(sre[:, n * kst:(n + 1) * kst].astype(BF16), cm_ref[0, 0, n], preferred_element_type=F32)
                 + jnp.dot(sim[:, n * kst:(n + 1) * kst].astype(BF16), cm_ref[0, 1, n], preferred_element_type=F32))
            oc = slice(n * kout, (n + 1) * kout)
            y = y + dsel[:, oc] * u[:, oc]
            y_ref[0, :, :, oc] = y.reshape(tl, B, kout)


def _s5_scan(u3, h0, lam, wb, cm, d, *, emit_y):
    L, B, W = u3.shape
    SW = lam.shape[2]
    tl = min(32, L)
    nT = L // tl
    tmap = lambda dr, i: i + dr * (nT - 1 - 2 * i)
    in_specs = [pl.BlockSpec((tl, B, W), lambda dr, i: (tmap(dr, i), 0, 0)),
                pl.BlockSpec((1, 2, B, SW), lambda dr, i: (dr, 0, 0, 0)),
                pl.BlockSpec((1, 2, SW), lambda dr, i: (dr, 0, 0)),
                pl.BlockSpec((1,) + wb.shape[1:], lambda dr, i: (dr, 0, 0, 0, 0)),
                pl.BlockSpec((1,) + cm.shape[1:], lambda dr, i: (dr, 0, 0, 0, 0)),
                pl.BlockSpec((1, W), lambda dr, i: (0, 0))]
    out_specs = [pl.BlockSpec((1, 2, B, SW), lambda dr, i: (dr, 0, 0, 0))]
    out_shape = [jax.ShapeDtypeStruct((2, 2, B, SW), F32)]
    if emit_y:
        out_specs = [pl.BlockSpec((1, tl, B, W), lambda dr, i: (dr, tmap(dr, i), 0, 0))] + out_specs
        out_shape = [jax.ShapeDtypeStruct((2, L, B, W), F32)] + out_shape
    res = pl.pallas_call(
        functools.partial(_s5_kernel, tl=tl, nb=nT, cw=256, emit_y=emit_y),
        grid=(2, nT),
        in_specs=in_specs, out_specs=out_specs, out_shape=out_shape,
        scratch_shapes=[pltpu.VMEM((tl * B, SW), F32), pltpu.VMEM((tl * B, SW), F32),
                        pltpu.VMEM((B, SW), F32), pltpu.VMEM((B, SW), F32)],
        compiler_params=_cparams(("arbitrary", "arbitrary")),
        name="s5_scan_x" if emit_y else "s5_scan_ctx",
    )(u3, h0, lam, wb, cm, d.reshape(1, W))
    return res if emit_y else (None, res[0])


def _hy_filter_kernel(w1t_ref, w1c_ref, w1s_ref, b1_ref, wh_ref, bh_ref, fr_ref, wout_ref, dec_ref,
                      bands_ref, hs_ref, hd_ref, *, L, hy_w):
    pos = lax.broadcasted_iota(jnp.int32, (L, 1), 0).astype(F32)
    t = pos / max(L - 1, 1)
    ang = (2.0 * math.pi / L) * pos * bands_ref[...]
    fr = fr_ref[...]
    dot = functools.partial(jnp.dot, preferred_element_type=F32, precision=HIGHEST)
    pre = t * w1t_ref[...] + dot(jnp.cos(ang), w1c_ref[...]) + dot(-jnp.sin(ang), w1s_ref[...]) + b1_ref[...]
    hid = jnp.sin(fr * pre)
    for i in range(wh_ref.shape[0]):
        hid = jnp.sin(fr * (dot(hid, wh_ref[i]) + bh_ref[i]))
    h = dot(hid, wout_ref[...]) * jnp.exp(-t * jnp.abs(dec_ref[...]))
    nrm = jnp.sum(h * h, axis=0, keepdims=True)
    nrm = nrm[:, :hy_w] + nrm[:, hy_w:]
    scale = lax.rsqrt(nrm + EPS)
    hf = h[:, :hy_w] * scale
    hb = h[:, hy_w:] * scale
    hb0 = jnp.where(lax.broadcasted_iota(jnp.int32, hb.shape, 0) == 0, 0.0, hb)
    hs_ref[...] = hf + hb0
    hd_ref[...] = hf - hb0


def _hy_filters(L, w1, b1, wh, bh, freq, wout, decay, hy_w, order):
    nb = (w1.shape[0] - 1) // 2
    hid = w1.shape[1]
    bands = jnp.linspace(1e-4, nb - 1, nb, dtype=F32).reshape(1, nb)
    full = lambda shp: pl.BlockSpec(shp, lambda o: (0,) * len(shp))
    return pl.pallas_call(
        functools.partial(_hy_filter_kernel, L=L, hy_w=hy_w),
        grid=(order,),
        in_specs=[full((1, hid)), full((nb, hid)), full((nb, hid)), full((1, hid)),
                  full(wh.shape), full((wh.shape[0], 1, hid)), full((1, hid)),
                  pl.BlockSpec((hid, 2 * hy_w), lambda o: (0, o)),
                  pl.BlockSpec((1, 2 * hy_w), lambda o: (0, o)),
                  full((1, nb))],
        out_specs=[pl.BlockSpec((L, hy_w), lambda o: (0, o))] * 2,
        out_shape=[jax.ShapeDtypeStruct((L, order * hy_w), F32)] * 2,
        compiler_params=_cparams(("arbitrary",)),
        name="hyena_filters",
    )(w1[0:1], w1[1:1 + nb], w1[1 + nb:], b1.reshape(1, hid), wh, bh.reshape(wh.shape[0], 1, hid),
      freq.reshape(1, hid), wout, decay.reshape(1, -1), bands)


def _dft_kernel(c32_ref, s32_ref, cb_ref, s1_ref, s2_ref, *, L, rb):
    i = pl.program_id(0)
    f = lax.broadcasted_iota(jnp.int32, (rb, L), 0) + i * rb
    t = lax.broadcasted_iota(jnp.int32, (rb, L), 1)
    k = (f * t) % (2 * L)
    ang = k.astype(F32) * (math.pi / L)
    c = jnp.cos(ang)
    s = -jnp.sin(ang)
    c32_ref[...] = c
    s32_ref[...] = s
    cb_ref[...] = c.astype(BF16)
    sign_t = jnp.where(t % 2 == 0, 1.0, -1.0)
    sign_f = jnp.where(f % 2 == 0, 1.0, -1.0)
    s1_ref[...] = jnp.where(f == 0, sign_t, s).astype(BF16)
    s2_ref[...] = jnp.where(t == 0, sign_f, s).astype(BF16)


def _dft_mats(L):
    rb = min(256, L)
    spec = pl.BlockSpec((rb, L), lambda i: (i, 0))
    return pl.pallas_call(
        functools.partial(_dft_kernel, L=L, rb=rb),
        grid=(L // rb,),
        in_specs=[],
        out_specs=[spec] * 5,
        out_shape=[jax.ShapeDtypeStruct((L, L), F32)] * 2 + [jax.ShapeDtypeStruct((L, L), BF16)] * 3,
        compiler_params=_cparams(("arbitrary",)),
        name="dft_mats",
    )()


def _spec_kernel(c_ref, s_ref, hs_ref, hd_ref, kr_ref, ki_ref, kn_ref, *, L, rb):
    i = pl.program_id(1)
    f = lax.broadcasted_iota(jnp.int32, (rb, 1), 0) + i * rb
    w = jnp.where(f == 0, 1.0, 2.0) * (1.0 / (2 * L))
    dot = functools.partial(jnp.dot, preferred_element_type=F32, precision=HIGHEST)
    hs = hs_ref[...]
    kr_ref[0] = w * dot(c_ref[...], hs)
    ki_ref[0] = w * dot(s_ref[...], hd_ref[...])
    sign = jnp.where(lax.broadcasted_iota(jnp.int32, (L, 1), 0) % 2 == 0, 1.0, -1.0)
    kn_ref[0] = jnp.sum(sign * hs, axis=0, keepdims=True) * (1.0 / (2 * L))


def _hy_spectra(c32, s32, hs, hd, hy_w, order):
    L = c32.shape[0]
    rb = min(256, L)
    mat = pl.BlockSpec((rb, L), lambda o, i: (i, 0))
    hsp = pl.BlockSpec((L, hy_w), lambda o, i: (0, o))
    return pl.pallas_call(
        functools.partial(_spec_kernel, L=L, rb=rb),
        grid=(order, L // rb),
        in_specs=[mat, mat, hsp, hsp],
        out_specs=[pl.BlockSpec((1, rb, hy_w), lambda o, i: (o, i, 0))] * 2
                  + [pl.BlockSpec((1, 1, hy_w), lambda o, i: (o, 0, 0))],
        out_shape=[jax.ShapeDtypeStruct((order, L, hy_w), F32)] * 2
                  + [jax.ShapeDtypeStruct((order, 1, hy_w), F32)],
        compiler_params=_cparams(("arbitrary", "arbitrary")),
        name="hyena_spectra",
    )(c32, s32, hs, hd)


def _hy_conv_kernel(v_ref, x1_ref, x2_ref, c_ref, s_ref, kr_ref, ki_ref, kn_ref, d_ref, o_ref,
                    zf, zb, pre, pim, *, rb, order):
    o = pl.program_id(1)
    ph = pl.program_id(2)
    blk = pl.program_id(3)
    rows = pl.ds(pl.multiple_of(blk * rb, rb), rb)

    @pl.when((o == 0) & (ph == 0) & (blk == 0))
    def _():
        v = v_ref[...]
        zf[...] = v
        zb[...] = v.astype(BF16)

    @pl.when(ph == 0)
    def _():
        z = zb[...]
        zr = jnp.dot(c_ref[...], z, preferred_element_type=F32)
        zi = jnp.dot(s_ref[0], z, preferred_element_type=F32)
        kr, ki = kr_ref[0], ki_ref[0]
        first = (lax.broadcasted_iota(jnp.int32, zr.shape, 0) == 0) & (blk == 0)
        pr = kr * zr - jnp.where(first, 0.0, ki * zi)
        pi = jnp.where(first, kn_ref[0] * zi, kr * zi + ki * zr)
        pre[rows, :] = pr.astype(BF16)
        pim[rows, :] = pi.astype(BF16)

    @pl.when(ph == 1)
    def _():
        y = (jnp.dot(c_ref[...], pre[...], preferred_element_type=F32)
             + jnp.dot(s_ref[0], pim[...], preferred_element_type=F32))
        zin = zf[rows, :]
        gate = jnp.where(o == 0, x1_ref[rows, :], x2_ref[rows, :])
        zn = gate * (y + d_ref[0] * zin)
        zf[rows, :] = zn
        zb[rows, :] = zn.astype(BF16)

        @pl.when(o == order - 1)
        def _():
            o_ref[rows, :] = zn


def _hy_conv(v, x1, x2, cb, s12, kr, ki, kn, d_hy, hy_w):
    L, BC = v.shape
    order = kr.shape[0]
    nc = 256
    rb = min(512, L)
    nblk = L // rb
    nct = hy_w // nc
    col = pl.BlockSpec((L, nc), lambda c, o, p, k: (0, c))
    kmap = lambda c, o, p, k: (o, k * (1 - p) + (nblk - 1) * p, c % nct)
    return pl.pallas_call(
        functools.partial(_hy_conv_kernel, rb=rb, order=order),
        grid=(BC // nc, order, 2, nblk),
        in_specs=[col, col, col,
                  pl.BlockSpec((rb, L), lambda c, o, p, k: (k, 0)),
                  pl.BlockSpec((1, rb, L), lambda c, o, p, k: (p, k, 0)),
                  pl.BlockSpec((1, rb, nc), kmap),
                  pl.BlockSpec((1, rb, nc), kmap),
                  pl.BlockSpec((1, 1, nc), lambda c, o, p, k: (o, 0, c % nct)),
                  pl.BlockSpec((1, 1, nc), lambda c, o, p, k: (o, 0, c % nct))],
        out_specs=col,
        out_shape=jax.ShapeDtypeStruct((L, BC), F32),
        scratch_shapes=[pltpu.VMEM((L, nc), F32), pltpu.VMEM((L, nc), BF16),
                        pltpu.VMEM((L, nc), BF16), pltpu.VMEM((L, nc), BF16)],
        compiler_params=_cparams(("arbitrary",) * 4),
        name="hyena_conv",
    )(v, x1, x2, cb, s12, kr, ki, kn, d_hy.reshape(order, 1, hy_w))


def _merge_kernel(y_ref, z_ref, x_ref, mod_ref, wg_ref, bg_ref, g5_ref, gh_ref, wo5_ref, woh_ref, g2_ref,
                  x1_ref, h2_ref):
    y = y_ref[0] + y_ref[1]
    gy = _gelu(y)
    gate = jnp.dot(gy.astype(BF16), wg_ref[...], preferred_element_type=F32) + bg_ref[...]
    s5 = gy * _sigmoid(gate)
    n5 = _rms(s5, g5_ref[...])
    nh = _rms(z_ref[...], gh_ref[...])
    mix = (jnp.dot(n5.astype(BF16), wo5_ref[...], preferred_element_type=F32)
           + jnp.dot(nh.astype(BF16), woh_ref[...], preferred_element_type=F32))
    x1 = x_ref[0] + mod_ref[0, 2:3, :] * mix
    x1_ref[0] = x1
    h2 = _rms(x1, g2_ref[...]) * (1.0 + mod_ref[0, 4:5, :]) + mod_ref[0, 3:4, :]
    h2_ref[0] = h2.astype(BF16)


def _merge(y, z, x, mod3, w_glu, b_glu, g5, gh, w_out, g2):
    B, L, D = x.shape
    W5 = w_glu.shape[0]
    WH = z.shape[1] // B
    tl = min(512, L)
    full = lambda shp: pl.BlockSpec(shp, lambda b_, t: (0,) * len(shp))
    return pl.pallas_call(
        _merge_kernel,
        grid=(B, L // tl),
        in_specs=[pl.BlockSpec((2, tl, W5), lambda b_, t: (0, t, b_)),
                  pl.BlockSpec((tl, WH), lambda b_, t: (t, b_)),
                  pl.BlockSpec((1, tl, D), lambda b_, t: (b_, t, 0)),
                  pl.BlockSpec((1, N_MOD, D), lambda b_, t: (b_, 0, 0)),
                  full((W5, W5)), full((1, W5)), full((1, W5)), full((1, WH)),
                  full((W5, D)), full((WH, D)), full((1, D))],
        out_specs=[pl.BlockSpec((1, tl, D), lambda b_, t: (b_, t, 0))] * 2,
        out_shape=[jax.ShapeDtypeStruct((B, L, D), F32), jax.ShapeDtypeStruct((B, L, D), BF16)],
        compiler_params=_cparams(("arbitrary", "arbitrary")),
        name="merge_heads",
    )(y, z, x, mod3, w_glu.astype(BF16), b_glu.reshape(1, W5), g5.reshape(1, W5), gh.reshape(1, WH),
      w_out[:W5].astype(BF16), w_out[W5:].astype(BF16), g2.reshape(1, D))


def _topk_desc(s, k):
    vals = []
    cur = s
    for _ in range(k):
        m = jnp.max(cur, axis=0, keepdims=True)
        vals.append(m)
        cur = jnp.where(cur == m, -jnp.inf, cur)
    return jnp.concatenate(vals, axis=0)


def _peer_kernel(h_ref, x1_ref, mod_ref, wq_ref, k1_ref, k2_ref, u_ref, v_ref, gf_ref, o_ref,
                 ea, eb, th, acc, gbuf, *, nb1, topk):
    e = pl.program_id(1)
    ne = pl.num_programs(1)
    H, NK, _ = k1_ref.shape
    T = h_ref.shape[1]
    nt_dims = (((1,), (1,)), ((), ()))

    @pl.when(e == 0)
    def _():
        acc[...] = jnp.zeros_like(acc)
        q = jnp.dot(h_ref[0], wq_ref[...], preferred_element_type=F32)
        dk = k1_ref.shape[2]
        for hh in range(H):
            q1 = q[:, (2 * hh) * dk:(2 * hh + 1) * dk]
            q2 = q[:, (2 * hh + 1) * dk:(2 * hh + 2) * dk]
            s1 = lax.dot_general(k1_ref[hh], q1, nt_dims, preferred_element_type=F32, precision=HIGHEST)
            s2 = lax.dot_general(k2_ref[hh], q2, nt_dims, preferred_element_type=F32, precision=HIGHEST)
            v1 = _topk_desc(s1, topk)
            v2 = _topk_desc(s2, topk)
            cands = [v1[a:a + 1] + v2[b:b + 1]
                     for a in range(topk) for b in range(topk) if (a + 1) * (b + 1) <= topk]
            sel = _topk_desc(jnp.concatenate(cands, axis=0), topk)
            m1, m2 = v1[0:1], v2[0:1]
            mx = sel[0:1]
            z = jnp.sum(jnp.exp(sel - mx), axis=0, keepdims=True)
            rz = 1.0 / z
            ea[hh] = jnp.exp(s1 - m1) * rz
            eb[hh] = jnp.exp(s2 - m2)
            th[hh] = jnp.exp(sel[topk - 1:topk] - mx) * rz * (1.0 - 2.0 ** -18)

    a = lax.dot_general(u_ref[...], h_ref[0], nt_dims, preferred_element_type=F32)
    for j in range(nb1):
        i1 = e * nb1 + j
        w = jnp.zeros((NK, T), F32)
        for hh in range(H):
            p = ea[hh, pl.ds(i1, 1), :] * eb[hh]
            w = w + jnp.where(p >= th[hh], p, 0.0)
        gbuf[j * NK:(j + 1) * NK, :] = (_gelu(a[j * NK:(j + 1) * NK, :]) * w).astype(BF16)
    acc[...] += lax.dot_general(gbuf[...], v_ref[...], (((0,), (0,)), ((), ())), preferred_element_type=F32)

    @pl.when(e == ne - 1)
    def _():
        x2 = x1_ref[0] + mod_ref[0, 5:6, :] * acc[...]
        o_ref[0] = _rms(x2, gf_ref[...])


def _peer(h2, x1, mod3, wq, k1, k2, u_tab, v_tab, g_final):
    B, L, D = x1.shape
    H, NK, dk = k1.shape
    NE = u_tab.shape[0]
    T = min(512, L)
    nb1 = 4
    nlt = L // T
    full = lambda shp: pl.BlockSpec(shp, lambda t, e: (0,) * len(shp))
    tok = pl.BlockSpec((1, T, D), lambda t, e: (t // nlt, t % nlt, 0))
    return pl.pallas_call(
        functools.partial(_peer_kernel, nb1=nb1, topk=PEER_TOPK),
        grid=(B * nlt, NK // nb1),
        in_specs=[tok, tok,
                  pl.BlockSpec((1, N_MOD, D), lambda t, e: (t // nlt, 0, 0)),
                  full(wq.shape), full(k1.shape), full(k2.shape),
                  pl.BlockSpec((nb1 * NK, D), lambda t, e: (e, 0)),
                  pl.BlockSpec((nb1 * NK, D), lambda t, e: (e, 0)),
                  full((1, D))],
        out_specs=tok,
        out_shape=jax.ShapeDtypeStruct((B, L, D), F32),
        scratch_shapes=[pltpu.VMEM((H, NK, T), F32), pltpu.VMEM((H, NK, T), F32),
                        pltpu.VMEM((H, 1, T), F32), pltpu.VMEM((T, D), F32),
                        pltpu.VMEM((nb1 * NK, T), BF16)],
        compiler_params=_cparams(("arbitrary", "arbitrary")),
        name="peer_dense",
    )(h2, x1, mod3, wq.astype(BF16), k1, k2, u_tab.astype(BF16), v_tab.astype(BF16), g_final.reshape(1, D))


def kernel(x, c, ctx, c_ctx, w_ada, b_ada, g_norm1, g_norm2, w_in, b_in, s5_a_re, s5_a_im, s5_log_step, s5_b_re, s5_b_im, s5_c_re, s5_c_im, s5_d, w_glu, b_glu, hy_conv_w, hy_conv_b, hf_w1, hf_b1, hf_wh, hf_bh, hf_freq, hf_wout, hf_decay, hy_d, g_out_s5, g_out_hy, w_out, peer_wq, peer_k1, peer_k2, peer_u, peer_v, g_final):
    B, L, D = x.shape
    Lc = ctx.shape[1]
    depth = w_ada.shape[0]
    assert depth == 1, "single-layer block"
    l = 0
    S5W = w_glu.shape[1]
    HYW = g_out_hy.shape[1]
    order = hy_d.shape[1]
    _, _, G, P, Hs = s5_b_re.shape

    rpad = (-(B + 1)) % 8
    cc = jnp.concatenate([c, c_ctx[None], jnp.zeros((rpad, D), F32)], axis=0)
    mod = _ada(cc, w_ada[l], b_ada[l])
    mod_x = mod[:B].reshape(B, N_MOD, D)
    mod_c = mod[B:B + 1].reshape(1, N_MOD, D)

    w_in_bf = w_in[l].astype(BF16)
    u_x, v, xg1, xg2 = _inproj(x, mod_x, g_norm1[l], w_in_bf, b_in[l], hy_conv_w[l], hy_conv_b[l],
                               s5_w=S5W, hy_w=HYW, hyena=True)
    (u_c,) = _inproj(ctx, mod_c, g_norm1[l], w_in_bf[:, :S5W], b_in[l][:S5W], None, None,
                     s5_w=S5W, hy_w=HYW, hyena=False)

    lr, li, bbr, bbi = _s5_prep(s5_a_re[l], s5_a_im[l], s5_log_step[l], s5_b_re[l], s5_b_im[l])
    gin = MXU_W // Hs
    lam = jnp.stack([lr.reshape(2, G * P), li.reshape(2, G * P)], axis=1)
    tr = lambda m: jnp.swapaxes(m, 2, 3)
    wb = jnp.stack([_block_diag(tr(bbr), gin), _block_diag(tr(bbi), gin)], axis=1).astype(BF16)
    cm = jnp.stack([_block_diag(tr(s5_c_re[l]), gin), _block_diag(-tr(s5_c_im[l]), gin)], axis=1).astype(BF16)
    h_zero = jnp.zeros((2, 2, B, G * P), F32)
    _, h_ctx = _s5_scan(u_c.reshape(Lc, B, S5W), h_zero, lam, wb, cm, s5_d[l], emit_y=False)
    y5, _ = _s5_scan(u_x.reshape(L, B, S5W), h_ctx, lam, wb, cm, s5_d[l], emit_y=True)

    hs, hd = _hy_filters(L, hf_w1[l], hf_b1[l], hf_wh[l], hf_bh[l], hf_freq[l], hf_wout[l], hf_decay[l],
                         HYW, order)
    c32, s32, cb, s1b, s2b = _dft_mats(L)
    kr, ki, kn = _hy_spectra(c32, s32, hs, hd, HYW, order)
    z_hy = _hy_conv(v, xg1, xg2, cb, jnp.stack([s1b, s2b]), kr, ki, kn, hy_d[l], HYW)

    x1, h2 = _merge(y5.reshape(2, L, B * S5W), z_hy, x, mod_x, w_glu[l], b_glu[l],
                    g_out_s5[l], g_out_hy[l], w_out[l], g_norm2[l])
    return _peer(h2, x1, mod_x, peer_wq[l], peer_k1[l], peer_k2[l], peer_u[l], peer_v[l], g_final)
```

```python
import functools
import math

import jax
import jax.numpy as jnp
from jax import lax
from jax.experimental import pallas as pl
from jax.experimental.pallas import tpu as pltpu

F32 = jnp.float32
BF16 = jnp.bfloat16
HIGHEST = lax.Precision.HIGHEST

EPS = 1e-6
GRID_W = 64
PEER_TOPK = 16
N_MOD = 6

MXU_W = 256
VMEM_LIMIT = 56 * 1024 * 1024


def _cparams(sem):
    return pltpu.CompilerParams(dimension_semantics=sem, vmem_limit_bytes=VMEM_LIMIT)


def _gelu(x):
    c = math.sqrt(2.0 / math.pi)
    return 0.5 * x * (1.0 + jnp.tanh(c * (x + 0.044715 * (x * x * x))))


def _sigmoid(x):
    return 1.0 / (1.0 + jnp.exp(-x))


def _rms(x, g):
    return x * lax.rsqrt(jnp.mean(x * x, axis=-1, keepdims=True) + EPS) * g


def _ada_kernel(c_ref, w_ref, b_ref, o_ref):
    c = c_ref[...]
    s = c * _sigmoid(c)
    o_ref[...] = jnp.dot(s, w_ref[...], preferred_element_type=F32, precision=HIGHEST) + b_ref[...]


def _ada(cc, w, b):
    R, D = cc.shape
    N = w.shape[1]
    tn = 1024
    return pl.pallas_call(
        _ada_kernel,
        grid=(N // tn,),
        in_specs=[pl.BlockSpec((R, D), lambda j: (0, 0)),
                  pl.BlockSpec((D, tn), lambda j: (0, j)),
                  pl.BlockSpec((1, tn), lambda j: (0, j))],
        out_specs=pl.BlockSpec((R, tn), lambda j: (0, j)),
        out_shape=jax.ShapeDtypeStruct((R, N), F32),
        compiler_params=_cparams(("arbitrary",)),
        name="ada_mod",
    )(cc, w, b.reshape(1, N))


def _inproj_kernel(x_ref, mod_ref, g_ref, w_ref, b_ref, *rest, s5_w, hy_w, hyena):
    x = x_ref[0]
    h = _rms(x, g_ref[...])
    h = h * (1.0 + mod_ref[0, 1:2, :]) + mod_ref[0, 0:1, :]
    p = jnp.dot(h.astype(BF16), w_ref[...], preferred_element_type=F32) + b_ref[...]
    if not hyena:
        (u_ref,) = rest
        u_ref[...] = p
        return
    cw_ref, cb_ref, u_ref, v_ref, x1_ref, x2_ref = rest
    u_ref[...] = p[:, :s5_w]
    ph = p[:, s5_w:]
    tl = ph.shape[0]
    pos = lax.broadcasted_iota(jnp.int32, ph.shape, 0) % GRID_W
    prev = jnp.where(pos == 0, 0.0, pltpu.roll(ph, 1, axis=0))
    nxt = jnp.where(pos == GRID_W - 1, 0.0, pltpu.roll(ph, tl - 1, axis=0))
    q = cb_ref[...] + cw_ref[0:1, :] * prev + cw_ref[1:2, :] * ph + cw_ref[2:3, :] * nxt
    v_ref[...] = q[:, :hy_w]
    x1_ref[...] = q[:, hy_w:2 * hy_w]
    x2_ref[...] = q[:, 2 * hy_w:]


def _inproj(x, mod3, g, w_bf, b, conv_w, conv_b, *, s5_w, hy_w, hyena):
    B, L, D = x.shape
    N = w_bf.shape[1]
    tl = min(512, L)
    nb = mod3.shape[0]
    mod_map = (lambda b_, t: (b_, 0, 0)) if nb > 1 else (lambda b_, t: (0, 0, 0))
    in_specs = [pl.BlockSpec((1, tl, D), lambda b_, t: (b_, t, 0)),
                pl.BlockSpec((1, N_MOD, D), mod_map),
                pl.BlockSpec((1, D), lambda b_, t: (0, 0)),
                pl.BlockSpec((D, N), lambda b_, t: (0, 0)),
                pl.BlockSpec((1, N), lambda b_, t: (0, 0))]
    args = [x, mod3, g.reshape(1, D), w_bf, b.reshape(1, N)]
    out_specs = [pl.BlockSpec((tl, s5_w), lambda b_, t: (t, b_))]
    out_shape = [jax.ShapeDtypeStruct((L, B * s5_w), F32)]
    if hyena:
        nh = N - s5_w
        in_specs += [pl.BlockSpec((3, nh), lambda b_, t: (0, 0)),
                     pl.BlockSpec((1, nh), lambda b_, t: (0, 0))]
        args += [conv_w, conv_b.reshape(1, nh)]
        out_specs += [pl.BlockSpec((tl, hy_w), lambda b_, t: (t, b_))] * 3
        out_shape += [jax.ShapeDtypeStruct((L, B * hy_w), F32)] * 3
    return pl.pallas_call(
        functools.partial(_inproj_kernel, s5_w=s5_w, hy_w=hy_w, hyena=hyena),
        grid=(B, L // tl),
        in_specs=in_specs, out_specs=out_specs, out_shape=out_shape,
        compiler_params=_cparams(("arbitrary", "arbitrary")),
        name="inproj_hy" if hyena else "inproj_ctx",
    )(*args)


def _s5_prep_kernel(are_ref, aim_ref, ls_ref, bre_ref, bim_ref, lr_ref, li_ref, bbr_ref, bbi_ref):
    a_re, a_im = are_ref[...], aim_ref[...]
    step = jnp.exp(ls_ref[...])
    er = jnp.exp(a_re * step)
    lr = er * jnp.cos(a_im * step)
    li = er * jnp.sin(a_im * step)
    nr, ni = lr - 1.0, li
    d2 = a_re * a_re + a_im * a_im
    kr = (nr * a_re + ni * a_im) / d2
    ki = (ni * a_re - nr * a_im) / d2
    br, bi = bre_ref[...], bim_ref[...]
    lr_ref[...] = lr
    li_ref[...] = li
    bbr_ref[...] = kr * br - ki * bi
    bbi_ref[...] = kr * bi + ki * br


def _s5_prep(a_re, a_im, log_step, b_re, b_im):
    two, G, P, H = b_re.shape
    n = two * G * P
    col = lambda a: a.reshape(n, 1)
    ls = jnp.broadcast_to(log_step[:, :, None], (two, G, P))
    outs = pl.pallas_call(
        _s5_prep_kernel,
        out_shape=[jax.ShapeDtypeStruct((n, 1), F32)] * 2 + [jax.ShapeDtypeStruct((n, H), F32)] * 2,
        name="s5_prep",
    )(col(a_re), col(a_im), col(ls), b_re.reshape(n, H), b_im.reshape(n, H))
    lr, li, bbr, bbi = outs
    return (lr.reshape(two, G, P), li.reshape(two, G, P),
            bbr.reshape(two, G, P, H), bbi.reshape(two, G, P, H))


def _block_diag(m, gb):
    two, G, R, C = m.shape
    m = m.reshape(two, G // gb, gb, R, C)
    eye = jnp.eye(gb, dtype=m.dtype)
    out = m[:, :, :, :, None, :] * eye[None, None, :, None, :, None]
    return out.reshape(two, G // gb, gb * R, gb * C)


def _s5_kernel(u_ref, h0_ref, lam_ref, wb_ref, cm_ref, d_ref, *rest, tl, nb, cw, emit_y):
    if emit_y:
        y_ref, hfin_ref, sre, sim, hre, him = rest
    else:
        hfin_ref, sre, sim, hre, him = rest
    dr = pl.program_id(0)
    i = pl.program_id(1)
    B = u_ref.shape[1]
    SW = sre.shape[1]
    nsb = wb_ref.shape[2]
    kin = wb_ref.shape[3]
    kst = wb_ref.shape[4]

    @pl.when(i == 0)
    def _():
        hre[...] = h0_ref[0, 0]
        him[...] = h0_ref[0, 1]

    u = u_ref[...].reshape(tl * B, u_ref.shape[2])
    ub = u.astype(BF16)
    for j in range(nsb):
        uj = ub[:, j * kin:(j + 1) * kin]
        sre[:, j * kst:(j + 1) * kst] = jnp.dot(uj, wb_ref[0, 0, j], preferred_element_type=F32)
        sim[:, j * kst:(j + 1) * kst] = jnp.dot(uj, wb_ref[0, 1, j], preferred_element_type=F32)

    for c in range(SW // cw):
        cols = slice(c * cw, (c + 1) * cw)
        lr = jnp.broadcast_to(lam_ref[0, 0:1, cols], (B, cw))
        li = jnp.broadcast_to(lam_ref[0, 1:2, cols], (B, cw))

        def body(s, carry, cols=cols, lr=lr, li=li):
            h_r, h_i = carry
            t = jnp.where(dr == 0, s, tl - 1 - s)
            row = pl.multiple_of(t * B, B)
            n_r = lr * h_r - li * h_i + sre[pl.ds(row, B), cols]
            n_i = lr * h_i + li * h_r + sim[pl.ds(row, B), cols]
            sre[pl.ds(row, B), cols] = n_r
            sim[pl.ds(row, B), cols] = n_i
            return n_r, n_i

        h_r, h_i = lax.fori_loop(0, tl, body, (hre[:, cols], him[:, cols]), unroll=2)
        hre[:, cols] = h_r
        him[:, cols] = h_i

    hfin_ref[0, 0] = hre[...]
    hfin_ref[0, 1] = him[...]

    if emit_y:
        kout = cm_ref.shape[4]
        dsel = d_ref[...] * jnp.where(dr == 0, 1.0, 0.0)
        for n in range(nsb):
            y = (jnp.dot(sre[:, n * kst:(n + 1) * kst].astype(BF16), cm_ref[0, 0, n], preferred_element_type=F32)
                 + jnp.dot(sim[:, n * kst:(n + 1) * kst].astype(BF16), cm_ref[0, 1, n], preferred_element_type=F32))
            oc = slice(n * kout, (n + 1) * kout)
            y = y + dsel[:, oc] * u[:, oc]
            y_ref[0, :, :, oc] = y.reshape(tl, B, kout)


def _s5_scan(u3, h0, lam, wb, cm, d, *, emit_y):
    L, B, W = u3.shape
    SW = lam.shape[2]
    tl = min(32, L)
    nT = L // tl
    tmap = lambda dr, i: i + dr * (nT - 1 - 2 * i)
    in_specs = [pl.BlockSpec((tl, B, W), lambda dr, i: (tmap(dr, i), 0, 0)),
                pl.BlockSpec((1, 2, B, SW), lambda dr, i: (dr, 0, 0, 0)),
                pl.BlockSpec((1, 2, SW), lambda dr, i: (dr, 0, 0)),
                pl.BlockSpec((1,) + wb.shape[1:], lambda dr, i: (dr, 0, 0, 0, 0)),
                pl.BlockSpec((1,) + cm.shape[1:], lambda dr, i: (dr, 0, 0, 0, 0)),
                pl.BlockSpec((1, W), lambda dr, i: (0, 0))]
    out_specs = [pl.BlockSpec((1, 2, B, SW), lambda dr, i: (dr, 0, 0, 0))]
    out_shape = [jax.ShapeDtypeStruct((2, 2, B, SW), F32)]
    if emit_y:
        out_specs = [pl.BlockSpec((1, tl, B, W), lambda dr, i: (dr, tmap(dr, i), 0, 0))] + out_specs
        out_shape = [jax.ShapeDtypeStruct((2, L, B, W), F32)] + out_shape
    res = pl.pallas_call(
        functools.partial(_s5_kernel, tl=tl, nb=nT, cw=256, emit_y=emit_y),
        grid=(2, nT),
        in_specs=in_specs, out_specs=out_specs, out_shape=out_shape,
        scratch_shapes=[pltpu.VMEM((tl * B, SW), F32), pltpu.VMEM((tl * B, SW), F32),
                        pltpu.VMEM((B, SW), F32), pltpu.VMEM((B, SW), F32)],
        compiler_params=_cparams(("arbitrary", "arbitrary")),
        name="s5_scan_x" if emit_y else "s5_scan_ctx",
    )(u3, h0, lam, wb, cm, d.reshape(1, W))
    return res if emit_y else (None, res[0])


def _hy_filter_kernel(w1t_ref, w1c_ref, w1s_ref, b1_ref, wh_ref, bh_ref, fr_ref, wout_ref, dec_ref,
                      bands_ref, hs_ref, hd_ref, *, L, hy_w):
    pos = lax.broadcasted_iota(jnp.int32, (L, 1), 0).astype(F32)
    t = pos / max(L - 1, 1)
    ang = (2.0 * math.pi / L) * pos * bands_ref[...]
    fr = fr_ref[...]
    dot = functools.partial(jnp.dot, preferred_element_type=F32, precision=HIGHEST)
    pre = t * w1t_ref[...] + dot(jnp.cos(ang), w1c_ref[...]) + dot(-jnp.sin(ang), w1s_ref[...]) + b1_ref[...]
    hid = jnp.sin(fr * pre)
    for i in range(wh_ref.shape[0]):
        hid = jnp.sin(fr * (dot(hid, wh_ref[i]) + bh_ref[i]))
    h = dot(hid, wout_ref[...]) * jnp.exp(-t * jnp.abs(dec_ref[...]))
    nrm = jnp.sum(h * h, axis=0, keepdims=True)
    nrm = nrm[:, :hy_w] + nrm[:, hy_w:]
    scale = lax.rsqrt(nrm + EPS)
    hf = h[:, :hy_w] * scale
    hb = h[:, hy_w:] * scale
    hb0 = jnp.where(lax.broadcasted_iota(jnp.int32, hb.shape, 0) == 0, 0.0, hb)
    hs_ref[...] = hf + hb0
    hd_ref[...] = hf - hb0


def _hy_filters(L, w1, b1, wh, bh, freq, wout, decay, hy_w, order):
    nb = (w1.shape[0] - 1) // 2
    hid = w1.shape[1]
    bands = jnp.linspace(1e-4, nb - 1, nb, dtype=F32).reshape(1, nb)
    full = lambda shp: pl.BlockSpec(shp, lambda o: (0,) * len(shp))
    return pl.pallas_call(
        functools.partial(_hy_filter_kernel, L=L, hy_w=hy_w),
        grid=(order,),
        in_specs=[full((1, hid)), full((nb, hid)), full((nb, hid)), full((1, hid)),
                  full(wh.shape), full((wh.shape[0], 1, hid)), full((1, hid)),
                  pl.BlockSpec((hid, 2 * hy_w), lambda o: (0, o)),
                  pl.BlockSpec((1, 2 * hy_w), lambda o: (0, o)),
                  full((1, nb))],
        out_specs=[pl.BlockSpec((L, hy_w), lambda o: (0, o))] * 2,
        out_shape=[jax.ShapeDtypeStruct((L, order * hy_w), F32)] * 2,
        compiler_params=_cparams(("arbitrary",)),
        name="hyena_filters",
    )(w1[0:1], w1[1:1 + nb], w1[1 + nb:], b1.reshape(1, hid), wh, bh.reshape(wh.shape[0], 1, hid),
      freq.reshape(1, hid), wout, decay.reshape(1, -1), bands)


def _dft_kernel(c32_ref, s32_ref, cb_ref, s1_ref, s2_ref, *, L, rb):
    i = pl.program_id(0)
    f = lax.broadcasted_iota(jnp.int32, (rb, L), 0) + i * rb
    t = lax.broadcasted_iota(jnp.int32, (rb, L), 1)
    k = (f * t) % (2 * L)
    ang = k.astype(F32) * (math.pi / L)
    c = jnp.cos(ang)
    s = -jnp.sin(ang)
    c32_ref[...] = c
    s32_ref[...] = s
    cb_ref[...] = c.astype(BF16)
    sign_t = jnp.where(t % 2 == 0, 1.0, -1.0)
    sign_f = jnp.where(f % 2 == 0, 1.0, -1.0)
    s1_ref[...] = jnp.where(f == 0, sign_t, s).astype(BF16)
    s2_ref[...] = jnp.where(t == 0, sign_f, s).astype(BF16)


def _dft_mats(L):
    rb = min(256, L)
    spec = pl.BlockSpec((rb, L), lambda i: (i, 0))
    return pl.pallas_call(
        functools.partial(_dft_kernel, L=L, rb=rb),
        grid=(L // rb,),
        in_specs=[],
        out_specs=[spec] * 5,
        out_shape=[jax.ShapeDtypeStruct((L, L), F32)] * 2 + [jax.ShapeDtypeStruct((L, L), BF16)] * 3,
        compiler_params=_cparams(("arbitrary",)),
        name="dft_mats",
    )()


def _spec_kernel(c_ref, s_ref, hs_ref, hd_ref, kr_ref, ki_ref, kn_ref, *, L, rb):
    i = pl.program_id(1)
    f = lax.broadcasted_iota(jnp.int32, (rb, 1), 0) + i * rb
    w = jnp.where(f == 0, 1.0, 2.0) * (1.0 / (2 * L))
    dot = functools.partial(jnp.dot, preferred_element_type=F32, precision=HIGHEST)
    hs = hs_ref[...]
    kr_ref[0] = w * dot(c_ref[...], hs)
    ki_ref[0] = w * dot(s_ref[...], hd_ref[...])
    sign = jnp.where(lax.broadcasted_iota(jnp.int32, (L, 1), 0) % 2 == 0, 1.0, -1.0)
    kn_ref[0] = jnp.sum(sign * hs, axis=0, keepdims=True) * (1.0 / (2 * L))


def _hy_spectra(c32, s32, hs, hd, hy_w, order):
    L = c32.shape[0]
    rb = min(256, L)
    mat = pl.BlockSpec((rb, L), lambda o, i: (i, 0))
    hsp = pl.BlockSpec((L, hy_w), lambda o, i: (0, o))
    return pl.pallas_call(
        functools.partial(_spec_kernel, L=L, rb=rb),
        grid=(order, L // rb),
        in_specs=[mat, mat, hsp, hsp],
        out_specs=[pl.BlockSpec((1, rb, hy_w), lambda o, i: (o, i, 0))] * 2
                  + [pl.BlockSpec((1, 1, hy_w), lambda o, i: (o, 0, 0))],
        out_shape=[jax.ShapeDtypeStruct((order, L, hy_w), F32)] * 2
                  + [jax.ShapeDtypeStruct((order, 1, hy_w), F32)],
        compiler_params=_cparams(("arbitrary", "arbitrary")),
        name="hyena_spectra",
    )(c32, s32, hs, hd)


def _hy_conv_kernel(v_ref, x1_ref, x2_ref, c_ref, s_ref, kr_ref, ki_ref, kn_ref, d_ref, o_ref,
                    zf, zb, pre, pim, *, rb, order):
    o = pl.program_id(1)
    ph = pl.program_id(2)
    blk = pl.program_id(3)
    rows = pl.ds(pl.multiple_of(blk * rb, rb), rb)

    @pl.when((o == 0) & (ph == 0) & (blk == 0))
    def _():
        v = v_ref[...]
        zf[...] = v
        zb[...] = v.astype(BF16)

    @pl.when(ph == 0)
    def _():
        z = zb[...]
        zr = jnp.dot(c_ref[...], z, preferred_element_type=F32)
        zi = jnp.dot(s_ref[0], z, preferred_element_type=F32)
        kr, ki = kr_ref[0], ki_ref[0]
        first = (lax.broadcasted_iota(jnp.int32, zr.shape, 0) == 0) & (blk == 0)
        pr = kr * zr - jnp.where(first, 0.0, ki * zi)
        pi = jnp.where(first, kn_ref[0] * zi, kr * zi + ki * zr)
        pre[rows, :] = pr.astype(BF16)
        pim[rows, :] = pi.astype(BF16)

    @pl.when(ph == 1)
    def _():
        y = (jnp.dot(c_ref[...], pre[...], preferred_element_type=F32)
             + jnp.dot(s_ref[0], pim[...], preferred_element_type=F32))
        zin = zf[rows, :]
        gate = jnp.where(o == 0, x1_ref[rows, :], x2_ref[rows, :])
        zn = gate * (y + d_ref[0] * zin)
        zf[rows, :] = zn
        zb[rows, :] = zn.astype(BF16)

        @pl.when(o == order - 1)
        def _():
            o_ref[rows, :] = zn


def _hy_conv(v, x1, x2, cb, s12, kr, ki, kn, d_hy, hy_w):
    L, BC = v.shape
    order = kr.shape[0]
    nc = 256
    rb = min(512, L)
    nblk = L // rb
    nct = hy_w // nc
    col = pl.BlockSpec((L, nc), lambda c, o, p, k: (0, c))
    kmap = lambda c, o, p, k: (o, k * (1 - p) + (nblk - 1) * p, c % nct)
    return pl.pallas_call(
        functools.partial(_hy_conv_kernel, rb=rb, order=order),
        grid=(BC // nc, order, 2, nblk),
        in_specs=[col, col, col,
                  pl.BlockSpec((rb, L), lambda c, o, p, k: (k, 0)),
                  pl.BlockSpec((1, rb, L), lambda c, o, p, k: (p, k, 0)),
                  pl.BlockSpec((1, rb, nc), kmap),
                  pl.BlockSpec((1, rb, nc), kmap),
                  pl.BlockSpec((1, 1, nc), lambda c, o, p, k: (o, 0, c % nct)),
                  pl.BlockSpec((1, 1, nc), lambda c, o, p, k: (o, 0, c % nct))],
        out_specs=col,
        out_shape=jax.ShapeDtypeStruct((L, BC), F32),
        scratch_shapes=[pltpu.VMEM((L, nc), F32), pltpu.VMEM((L, nc), BF16),
                        pltpu.VMEM((L, nc), BF16), pltpu.VMEM((L, nc), BF16)],
        compiler_params=_cparams(("arbitrary",) * 4),
        name="hyena_conv",
    )(v, x1, x2, cb, s12, kr, ki, kn, d_hy.reshape(order, 1, hy_w))


def _merge_kernel(y_ref, z_ref, x_ref, mod_ref, wg_ref, bg_ref, g5_ref, gh_ref, wo5_ref, woh_ref, g2_ref,
                  x1_ref, h2_ref):
    y = y_ref[0] + y_ref[1]
    gy = _gelu(y)
    gate = jnp.dot(gy.astype(BF16), wg_ref[...], preferred_element_type=F32) + bg_ref[...]
    s5 = gy * _sigmoid(gate)
    n5 = _rms(s5, g5_ref[...])
    nh = _rms(z_ref[...], gh_ref[...])
    mix = (jnp.dot(n5.astype(BF16), wo5_ref[...], preferred_element_type=F32)
           + jnp.dot(nh.astype(BF16), woh_ref[...], preferred_element_type=F32))
    x1 = x_ref[0] + mod_ref[0, 2:3, :] * mix
    x1_ref[0] = x1
    h2 = _rms(x1, g2_ref[...]) * (1.0 + mod_ref[0, 4:5, :]) + mod_ref[0, 3:4, :]
    h2_ref[0] = h2.astype(BF16)


def _merge(y, z, x, mod3, w_glu, b_glu, g5, gh, w_out, g2):
    B, L, D = x.shape
    W5 = w_glu.shape[0]
    WH = z.shape[1] // B
    tl = min(512, L)
    full = lambda shp: pl.BlockSpec(shp, lambda b_, t: (0,) * len(shp))
    return pl.pallas_call(
        _merge_kernel,
        grid=(B, L // tl),
        in_specs=[pl.BlockSpec((2, tl, W5), lambda b_, t: (0, t, b_)),
                  pl.BlockSpec((tl, WH), lambda b_, t: (t, b_)),
                  pl.BlockSpec((1, tl, D), lambda b_, t: (b_, t, 0)),
                  pl.BlockSpec((1, N_MOD, D), lambda b_, t: (b_, 0, 0)),
                  full((W5, W5)), full((1, W5)), full((1, W5)), full((1, WH)),
                  full((W5, D)), full((WH, D)), full((1, D))],
        out_specs=[pl.BlockSpec((1, tl, D), lambda b_, t: (b_, t, 0))] * 2,
        out_shape=[jax.ShapeDtypeStruct((B, L, D), F32), jax.ShapeDtypeStruct((B, L, D), BF16)],
        compiler_params=_cparams(("arbitrary", "arbitrary")),
        name="merge_heads",
    )(y, z, x, mod3, w_glu.astype(BF16), b_glu.reshape(1, W5), g5.reshape(1, W5), gh.reshape(1, WH),
      w_out[:W5].astype(BF16), w_out[W5:].astype(BF16), g2.reshape(1, D))


def _topk_rank(s, k):
    vals = []
    cur = s
    rank = jnp.full(s.shape, float(k), F32)
    for i in range(k):
        m = jnp.max(cur, axis=0, keepdims=True)
        hit = cur == m
        vals.append(m)
        cur = jnp.where(hit, -jnp.inf, cur)
        rank = jnp.where(hit, float(i), rank)
    return jnp.concatenate(vals, axis=0), rank


def _topk_desc(s, k):
    vals = []
    cur = s
    for _ in range(k):
        m = jnp.max(cur, axis=0, keepdims=True)
        vals.append(m)
        cur = jnp.where(cur == m, -jnp.inf, cur)
    return jnp.concatenate(vals, axis=0)


def _gelu_packed(x):
    c = math.sqrt(2.0 / math.pi)
    inner = x * (c + (c * 0.044715) * (x * x))
    hx = 0.5 * x
    return hx + hx * jnp.tanh(inner)


PACK = 16


def _peer_kernel(h_ref, x1_ref, mod_ref, wq_ref, k1_ref, k2_ref, u_ref, vt_ref, gf_ref, o_ref,
                 ea, cs, eb, r2b, acc, a0, a1, g0, g1, *, nb1, topk):
    s = pl.program_id(1)
    ne = pl.num_programs(1) - 2
    H, NK, _ = k1_ref.shape
    T = h_ref.shape[1]
    nt_dims = (((1,), (1,)), ((), ()))

    @pl.when(s == 0)
    def _():
        acc[...] = jnp.zeros_like(acc)
        for buf in (a0, a1, g0, g1):
            buf[...] = jnp.zeros_like(buf)
        q = jnp.dot(h_ref[0], wq_ref[...], preferred_element_type=F32)
        dk = k1_ref.shape[2]
        for hh in range(H):
            q1 = q[:, (2 * hh) * dk:(2 * hh + 1) * dk]
            q2 = q[:, (2 * hh + 1) * dk:(2 * hh + 2) * dk]
            s1 = lax.dot_general(k1_ref[hh], q1, nt_dims, preferred_element_type=F32, precision=HIGHEST)
            s2 = lax.dot_general(k2_ref[hh], q2, nt_dims, preferred_element_type=F32, precision=HIGHEST)
            v1, r1 = _topk_rank(s1, topk)
            v2, r2 = _topk_rank(s2, topk)
            cands = [v1[a:a + 1] + v2[b:b + 1]
                     for a in range(topk) for b in range(topk) if (a + 1) * (b + 1) <= topk]
            sel = _topk_desc(jnp.concatenate(cands, axis=0), topk)
            tau = sel[topk - 1:topk]
            mx = sel[0:1]
            rz = 1.0 / jnp.sum(jnp.exp(sel - mx), axis=0, keepdims=True)
            cnt = jnp.zeros((NK, T), F32)
            for a_ in range(topk):
                c_a = jnp.sum(jnp.where(v1[a_:a_ + 1] + v2 >= tau, 1.0, 0.0), axis=0, keepdims=True)
                cnt = jnp.where(r1 == float(a_), c_a, cnt)
            ea[hh] = jnp.exp(s1 - v1[0:1]) * rz
            cs[hh] = cnt
            eb[hh] = jnp.exp(s2 - v2[0:1]).astype(BF16)
            r2b[hh] = r2.astype(BF16)

    e_mid = jnp.clip(s - 1, 0, ne - 1)

    def stages(a_new, a_old, g_new, g_old):
        th = T // 2

        def mxu_piece(p):
            tc = slice((p // 2) * th, (p // 2 + 1) * th)
            if p % 2 == 0:
                a_new[:, tc] = lax.dot_general(u_ref[...], h_ref[0, tc, :], nt_dims,
                                               preferred_element_type=F32)
            else:
                acc[:, tc] += jnp.dot(vt_ref[...], g_old[:, tc], preferred_element_type=F32)

        row_cache = {}

        def packed_rows(j):
            if j not in row_cache:
                i1 = e_mid * nb1 + j
                row_cache[j] = tuple(
                    [jnp.broadcast_to(src[hh, pl.ds(i1, 1), :], (PACK, T)).astype(BF16) for hh in range(H)]
                    for src in (ea, cs))
            return row_cache[j]

        def vpu_part(j, r_lo, r_hi):
            ea_p, cs_p = packed_rows(j)
            for r in range(r_lo, r_hi):
                rs = slice(r * PACK, (r + 1) * PACK)
                w = jnp.zeros((PACK, T), BF16)
                for hh in range(H):
                    w = w + jnp.where(r2b[hh, rs, :] < cs_p[hh], ea_p[hh] * eb[hh, rs, :],
                                      jnp.zeros((), BF16))
                rows = slice(j * NK + r * PACK, j * NK + (r + 1) * PACK)
                g_new[rows, :] = _gelu_packed(a_old[rows, :].astype(BF16)) * w

        assert nb1 == 4
        nr = NK // PACK
        for p in range(4):
            mxu_piece(p)
            vpu_part(p, 0, nr)

    @pl.when(s % 2 == 0)
    def _():
        stages(a0, a1, g1, g0)

    @pl.when(s % 2 == 1)
    def _():
        stages(a1, a0, g0, g1)

    @pl.when(s == ne + 1)
    def _():
        x2 = x1_ref[0] + mod_ref[0, 5:6, :] * acc[...].T
        o_ref[0] = _rms(x2, gf_ref[...])


def _peer(h2, x1, mod3, wq, k1, k2, u_tab, v_tab, g_final):
    B, L, D = x1.shape
    H, NK, dk = k1.shape
    NE = u_tab.shape[0]
    T = min(512, L)
    nb1 = 4
    nlt = L // T
    ne = NK // nb1
    full = lambda shp: pl.BlockSpec(shp, lambda t, s: (0,) * len(shp))
    tok = pl.BlockSpec((1, T, D), lambda t, s: (t // nlt, t % nlt, 0))
    return pl.pallas_call(
        functools.partial(_peer_kernel, nb1=nb1, topk=PEER_TOPK),
        grid=(B * nlt, ne + 2),
        in_specs=[tok, tok,
                  pl.BlockSpec((1, N_MOD, D), lambda t, s: (t // nlt, 0, 0)),
                  full(wq.shape), full(k1.shape), full(k2.shape),
                  pl.BlockSpec((nb1 * NK, D), lambda t, s: (jnp.minimum(s, ne - 1), 0)),
                  pl.BlockSpec((D, nb1 * NK), lambda t, s: (0, jnp.clip(s - 2, 0, ne - 1))),
                  full((1, D))],
        out_specs=tok,
        out_shape=jax.ShapeDtypeStruct((B, L, D), F32),
        scratch_shapes=[pltpu.VMEM((H, NK, T), F32), pltpu.VMEM((H, NK, T), F32),
                        pltpu.VMEM((H, NK, T), BF16), pltpu.VMEM((H, NK, T), BF16),
                        pltpu.VMEM((D, T), F32),
                        pltpu.VMEM((nb1 * NK, T), F32), pltpu.VMEM((nb1 * NK, T), F32),
                        pltpu.VMEM((nb1 * NK, T), BF16), pltpu.VMEM((nb1 * NK, T), BF16)],
        compiler_params=_cparams(("arbitrary", "arbitrary")),
        name="peer_dense",
    )(h2, x1, mod3, wq.astype(BF16), k1, k2, u_tab.astype(BF16), v_tab.astype(BF16).T, g_final.reshape(1, D))


def kernel(x, c, ctx, c_ctx, w_ada, b_ada, g_norm1, g_norm2, w_in, b_in, s5_a_re, s5_a_im, s5_log_step, s5_b_re, s5_b_im, s5_c_re, s5_c_im, s5_d, w_glu, b_glu, hy_conv_w, hy_conv_b, hf_w1, hf_b1, hf_wh, hf_bh, hf_freq, hf_wout, hf_decay, hy_d, g_out_s5, g_out_hy, w_out, peer_wq, peer_k1, peer_k2, peer_u, peer_v, g_final):
    B, L, D = x.shape
    Lc = ctx.shape[1]
    depth = w_ada.shape[0]
    assert depth == 1, "single-layer block"
    l = 0
    S5W = w_glu.shape[1]
    HYW = g_out_hy.shape[1]
    order = hy_d.shape[1]
    _, _, G, P, Hs = s5_b_re.shape

    rpad = (-(B + 1)) % 8
    cc = jnp.concatenate([c, c_ctx[None], jnp.zeros((rpad, D), F32)], axis=0)
    mod = _ada(cc, w_ada[l], b_ada[l])
    mod_x = mod[:B].reshape(B, N_MOD, D)
    mod_c = mod[B:B + 1].reshape(1, N_MOD, D)

    w_in_bf = w_in[l].astype(BF16)
    u_x, v, xg1, xg2 = _inproj(x, mod_x, g_norm1[l], w_in_bf, b_in[l], hy_conv_w[l], hy_conv_b[l],
                               s5_w=S5W, hy_w=HYW, hyena=True)
    (u_c,) = _inproj(ctx, mod_c, g_norm1[l], w_in_bf[:, :S5W], b_in[l][:S5W], None, None,
                     s5_w=S5W, hy_w=HYW, hyena=False)

    lr, li, bbr, bbi = _s5_prep(s5_a_re[l], s5_a_im[l], s5_log_step[l], s5_b_re[l], s5_b_im[l])
    gin = MXU_W // Hs
    lam = jnp.stack([lr.reshape(2, G * P), li.reshape(2, G * P)], axis=1)
    tr = lambda m: jnp.swapaxes(m, 2, 3)
    wb = jnp.stack([_block_diag(tr(bbr), gin), _block_diag(tr(bbi), gin)], axis=1).astype(BF16)
    cm = jnp.stack([_block_diag(tr(s5_c_re[l]), gin), _block_diag(-tr(s5_c_im[l]), gin)], axis=1).astype(BF16)
    h_zero = jnp.zeros((2, 2, B, G * P), F32)
    _, h_ctx = _s5_scan(u_c.reshape(Lc, B, S5W), h_zero, lam, wb, cm, s5_d[l], emit_y=False)
    y5, _ = _s5_scan(u_x.reshape(L, B, S5W), h_ctx, lam, wb, cm, s5_d[l], emit_y=True)

    hs, hd = _hy_filters(L, hf_w1[l], hf_b1[l], hf_wh[l], hf_bh[l], hf_freq[l], hf_wout[l], hf_decay[l],
                         HYW, order)
    c32, s32, cb, s1b, s2b = _dft_mats(L)
    kr, ki, kn = _hy_spectra(c32, s32, hs, hd, HYW, order)
    z_hy = _hy_conv(v, xg1, xg2, cb, jnp.stack([s1b, s2b]), kr, ki, kn, hy_d[l], HYW)

    x1, h2 = _merge(y5.reshape(2, L, B * S5W), z_hy, x, mod_x, w_glu[l], b_glu[l],
                    g_out_s5[l], g_out_hy[l], w_out[l], g_norm2[l])
    return _peer(h2, x1, mod_x, peer_wq[l], peer_k1[l], peer_k2[l], peer_u[l], peer_v[l], g_final)
```

```python
import functools
import math

import jax
import jax.numpy as jnp
from jax import lax
from jax.experimental import pallas as pl
from jax.experimental.pallas import tpu as pltpu

F32 = jnp.float32
BF16 = jnp.bfloat16
HIGHEST = lax.Precision.HIGHEST

EPS = 1e-6
GRID_W = 64
PEER_TOPK = 16
N_MOD = 6

MXU_W = 256
MXU_ROWS = 256
VMEM_LIMIT = 56 * 1024 * 1024


def _cparams(sem):
    return pltpu.CompilerParams(dimension_semantics=sem, vmem_limit_bytes=VMEM_LIMIT)


def _gelu(x):
    c = math.sqrt(2.0 / math.pi)
    return 0.5 * x * (1.0 + jnp.tanh(c * (x + 0.044715 * (x * x * x))))


def _sigmoid(x):
    return 1.0 / (1.0 + jnp.exp(-x))


def _rms(x, g):
    return x * lax.rsqrt(jnp.mean(x * x, axis=-1, keepdims=True) + EPS) * g


def _ada_kernel(c_ref, w_ref, b_ref, o_ref):
    c = c_ref[...]
    s = c * _sigmoid(c)
    o_ref[...] = jnp.dot(s, w_ref[...], preferred_element_type=F32, precision=HIGHEST) + b_ref[...]


def _ada(cc, w, b):
    R, D = cc.shape
    N = w.shape[1]
    tn = 1024
    return pl.pallas_call(
        _ada_kernel,
        grid=(N // tn,),
        in_specs=[pl.BlockSpec((R, D), lambda j: (0, 0)),
                  pl.BlockSpec((D, tn), lambda j: (0, j)),
                  pl.BlockSpec((1, tn), lambda j: (0, j))],
        out_specs=pl.BlockSpec((R, tn), lambda j: (0, j)),
        out_shape=jax.ShapeDtypeStruct((R, N), F32),
        compiler_params=_cparams(("arbitrary",)),
        name="ada_mod",
    )(cc, w, b.reshape(1, N))


def _inproj_kernel(x_ref, mod_ref, g_ref, w_ref, b_ref, *rest, s5_w, hy_w, hyena):
    x = x_ref[0]
    h = _rms(x, g_ref[...])
    h = h * (1.0 + mod_ref[0, 1:2, :]) + mod_ref[0, 0:1, :]
    p = jnp.dot(h.astype(BF16), w_ref[...], preferred_element_type=F32) + b_ref[...]
    if not hyena:
        (u_ref,) = rest
        u_ref[...] = p
        return
    cw_ref, cb_ref, u_ref, v_ref, x1_ref, x2_ref = rest
    u_ref[...] = p[:, :s5_w]
    ph = p[:, s5_w:]
    tl = ph.shape[0]
    pos = lax.broadcasted_iota(jnp.int32, ph.shape, 0) % GRID_W
    prev = jnp.where(pos == 0, 0.0, pltpu.roll(ph, 1, axis=0))
    nxt = jnp.where(pos == GRID_W - 1, 0.0, pltpu.roll(ph, tl - 1, axis=0))
    q = cb_ref[...] + cw_ref[0:1, :] * prev + cw_ref[1:2, :] * ph + cw_ref[2:3, :] * nxt
    v_ref[...] = q[:, :hy_w]
    x1_ref[...] = q[:, hy_w:2 * hy_w]
    x2_ref[...] = q[:, 2 * hy_w:]


def _inproj(x, mod3, g, w_bf, b, conv_w, conv_b, *, s5_w, hy_w, hyena):
    B, L, D = x.shape
    N = w_bf.shape[1]
    tl = min(512, L)
    nb = mod3.shape[0]
    mod_map = (lambda b_, t: (b_, 0, 0)) if nb > 1 else (lambda b_, t: (0, 0, 0))
    in_specs = [pl.BlockSpec((1, tl, D), lambda b_, t: (b_, t, 0)),
                pl.BlockSpec((1, N_MOD, D), mod_map),
                pl.BlockSpec((1, D), lambda b_, t: (0, 0)),
                pl.BlockSpec((D, N), lambda b_, t: (0, 0)),
                pl.BlockSpec((1, N), lambda b_, t: (0, 0))]
    args = [x, mod3, g.reshape(1, D), w_bf, b.reshape(1, N)]
    out_specs = [pl.BlockSpec((tl, s5_w), lambda b_, t: (t, b_))]
    out_shape = [jax.ShapeDtypeStruct((L, B * s5_w), F32)]
    if hyena:
        nh = N - s5_w
        in_specs += [pl.BlockSpec((3, nh), lambda b_, t: (0, 0)),
                     pl.BlockSpec((1, nh), lambda b_, t: (0, 0))]
        args += [conv_w, conv_b.reshape(1, nh)]
        out_specs += [pl.BlockSpec((tl, hy_w), lambda b_, t: (t, b_))] * 3
        out_shape += [jax.ShapeDtypeStruct((L, B * hy_w), F32)] * 3
    return pl.pallas_call(
        functools.partial(_inproj_kernel, s5_w=s5_w, hy_w=hy_w, hyena=hyena),
        grid=(B, L // tl),
        in_specs=in_specs, out_specs=out_specs, out_shape=out_shape,
        compiler_params=_cparams(("arbitrary", "arbitrary")),
        name="inproj_hy" if hyena else "inproj_ctx",
    )(*args)


def _s5_prep_kernel(are_ref, aim_ref, ls_ref, bre_ref, bim_ref, lr_ref, li_ref, bbr_ref, bbi_ref):
    a_re, a_im = are_ref[...], aim_ref[...]
    step = jnp.exp(ls_ref[...])
    er = jnp.exp(a_re * step)
    lr = er * jnp.cos(a_im * step)
    li = er * jnp.sin(a_im * step)
    nr, ni = lr - 1.0, li
    d2 = a_re * a_re + a_im * a_im
    kr = (nr * a_re + ni * a_im) / d2
    ki = (ni * a_re - nr * a_im) / d2
    br, bi = bre_ref[...], bim_ref[...]
    lr_ref[...] = lr
    li_ref[...] = li
    bbr_ref[...] = kr * br - ki * bi
    bbi_ref[...] = kr * bi + ki * br


def _s5_prep(a_re, a_im, log_step, b_re, b_im):
    two, G, P, H = b_re.shape
    n = two * G * P
    col = lambda a: a.reshape(n, 1)
    ls = jnp.broadcast_to(log_step[:, :, None], (two, G, P))
    outs = pl.pallas_call(
        _s5_prep_kernel,
        out_shape=[jax.ShapeDtypeStruct((n, 1), F32)] * 2 + [jax.ShapeDtypeStruct((n, H), F32)] * 2,
        name="s5_prep",
    )(col(a_re), col(a_im), col(ls), b_re.reshape(n, H), b_im.reshape(n, H))
    lr, li, bbr, bbi = outs
    return (lr.reshape(two, G, P), li.reshape(two, G, P),
            bbr.reshape(two, G, P, H), bbi.reshape(two, G, P, H))


def _block_diag(m, gb):
    two, G, R, C = m.shape
    m = m.reshape(two, G // gb, gb, R, C)
    eye = jnp.eye(gb, dtype=m.dtype)
    out = m[:, :, :, :, None, :] * eye[None, None, :, None, :, None]
    return out.reshape(two, G // gb, gb * R, gb * C)


def _s5_kernel(u_ref, h0_ref, lam_ref, wb_ref, cm_ref, d_ref, *rest, tl, nb, cw, emit_y):
    if emit_y:
        y_ref, hfin_ref, sre, sim, hre, him = rest
    else:
        hfin_ref, sre, sim, hre, him = rest
    dr = pl.program_id(0)
    i = pl.program_id(1)
    B = u_ref.shape[1]
    SW = sre.shape[1]
    nsb = wb_ref.shape[2]
    kin = wb_ref.shape[3]
    kst = wb_ref.shape[4]

    @pl.when(i == 0)
    def _():
        hre[...] = h0_ref[0, 0]
        him[...] = h0_ref[0, 1]

    u = u_ref[...].reshape(tl * B, u_ref.shape[2])
    ub = u.astype(BF16)
    for j in range(nsb):
        uj = ub[:, j * kin:(j + 1) * kin]
        sre[:, j * kst:(j + 1) * kst] = jnp.dot(uj, wb_ref[0, 0, j], preferred_element_type=F32)
        sim[:, j * kst:(j + 1) * kst] = jnp.dot(uj, wb_ref[0, 1, j], preferred_element_type=F32)

    for c in range(SW // cw):
        cols = slice(c * cw, (c + 1) * cw)
        lr = jnp.broadcast_to(lam_ref[0, 0:1, cols], (B, cw))
        li = jnp.broadcast_to(lam_ref[0, 1:2, cols], (B, cw))

        def body(s, carry, cols=cols, lr=lr, li=li):
            h_r, h_i = carry
            t = jnp.where(dr == 0, s, tl - 1 - s)
            row = pl.multiple_of(t * B, B)
            n_r = lr * h_r - li * h_i + sre[pl.ds(row, B), cols]
            n_i = lr * h_i + li * h_r + sim[pl.ds(row, B), cols]
            sre[pl.ds(row, B), cols] = n_r
            sim[pl.ds(row, B), cols] = n_i
            return n_r, n_i

        h_r, h_i = lax.fori_loop(0, tl, body, (hre[:, cols], him[:, cols]), unroll=2)
        hre[:, cols] = h_r
        him[:, cols] = h_i

    hfin_ref[0, 0] = hre[...]
    hfin_ref[0, 1] = him[...]

    if emit_y:
        kout = cm_ref.shape[4]
        dsel = d_ref[...] * jnp.where(dr == 0, 1.0, 0.0)
        for n in range(nsb):
            y = (jnp.dot(sre[:, n * kst:(n + 1) * kst].astype(BF16), cm_ref[0, 0, n], preferred_element_type=F32)
                 + jnp.dot(sim[:, n * kst:(n + 1) * kst].astype(BF16), cm_ref[0, 1, n], preferred_element_type=F32))
            oc = slice(n * kout, (n + 1) * kout)
            y = y + dsel[:, oc] * u[:, oc]
            y_ref[0, :, :, oc] = y.reshape(tl, B, kout)


def _s5_scan(u3, h0, lam, wb, cm, d, *, emit_y):
    L, B, W = u3.shape
    SW = lam.shape[2]
    tl = min(32, L)
    nT = L // tl
    tmap = lambda dr, i: i + dr * (nT - 1 - 2 * i)
    in_specs = [pl.BlockSpec((tl, B, W), lambda dr, i: (tmap(dr, i), 0, 0)),
                pl.BlockSpec((1, 2, B, SW), lambda dr, i: (dr, 0, 0, 0)),
                pl.BlockSpec((1, 2, SW), lambda dr, i: (dr, 0, 0)),
                pl.BlockSpec((1,) + wb.shape[1:], lambda dr, i: (dr, 0, 0, 0, 0)),
                pl.BlockSpec((1,) + cm.shape[1:], lambda dr, i: (dr, 0, 0, 0, 0)),
                pl.BlockSpec((1, W), lambda dr, i: (0, 0))]
    out_specs = [pl.BlockSpec((1, 2, B, SW), lambda dr, i: (dr, 0, 0, 0))]
    out_shape = [jax.ShapeDtypeStruct((2, 2, B, SW), F32)]
    if emit_y:
        out_specs = [pl.BlockSpec((1, tl, B, W), lambda dr, i: (dr, tmap(dr, i), 0, 0))] + out_specs
        out_shape = [jax.ShapeDtypeStruct((2, L, B, W), F32)] + out_shape
    res = pl.pallas_call(
        functools.partial(_s5_kernel, tl=tl, nb=nT, cw=256, emit_y=emit_y),
        grid=(2, nT),
        in_specs=in_specs, out_specs=out_specs, out_shape=out_shape,
        scratch_shapes=[pltpu.VMEM((tl * B, SW), F32), pltpu.VMEM((tl * B, SW), F32),
                        pltpu.VMEM((B, SW), F32), pltpu.VMEM((B, SW), F32)],
        compiler_params=_cparams(("arbitrary", "arbitrary")),
        name="s5_scan_x" if emit_y else "s5_scan_ctx",
    )(u3, h0, lam, wb, cm, d.reshape(1, W))
    return res if emit_y else (None, res[0])


def _hy_filter_kernel(w1t_ref, w1c_ref, w1s_ref, b1_ref, wh_ref, bh_ref, fr_ref, wout_ref, dec_ref,
                      bands_ref, hs_ref, hd_ref, *, L, hy_w):
    pos = lax.broadcasted_iota(jnp.int32, (L, 1), 0).astype(F32)
    t = pos / max(L - 1, 1)
    ang = (2.0 * math.pi / L) * pos * bands_ref[...]
    fr = fr_ref[...]
    dot = functools.partial(jnp.dot, preferred_element_type=F32, precision=HIGHEST)
    pre = t * w1t_ref[...] + dot(jnp.cos(ang), w1c_ref[...]) + dot(-jnp.sin(ang), w1s_ref[...]) + b1_ref[...]
    hid = jnp.sin(fr * pre)
    for i in range(wh_ref.shape[0]):
        hid = jnp.sin(fr * (dot(hid, wh_ref[i]) + bh_ref[i]))
    h = dot(hid, wout_ref[...]) * jnp.exp(-t * jnp.abs(dec_ref[...]))
    nrm = jnp.sum(h * h, axis=0, keepdims=True)
    nrm = nrm[:, :hy_w] + nrm[:, hy_w:]
    scale = lax.rsqrt(nrm + EPS)
    hf = h[:, :hy_w] * scale
    hb = h[:, hy_w:] * scale
    hb0 = jnp.where(lax.broadcasted_iota(jnp.int32, hb.shape, 0) == 0, 0.0, hb)
    hs_ref[...] = hf + hb0
    hd_ref[...] = hf - hb0


def _hy_filters(L, w1, b1, wh, bh, freq, wout, decay, hy_w, order):
    nb = (w1.shape[0] - 1) // 2
    hid = w1.shape[1]
    bands = jnp.linspace(1e-4, nb - 1, nb, dtype=F32).reshape(1, nb)
    full = lambda shp: pl.BlockSpec(shp, lambda o: (0,) * len(shp))
    return pl.pallas_call(
        functools.partial(_hy_filter_kernel, L=L, hy_w=hy_w),
        grid=(order,),
        in_specs=[full((1, hid)), full((nb, hid)), full((nb, hid)), full((1, hid)),
                  full(wh.shape), full((wh.shape[0], 1, hid)), full((1, hid)),
                  pl.BlockSpec((hid, 2 * hy_w), lambda o: (0, o)),
                  pl.BlockSpec((1, 2 * hy_w), lambda o: (0, o)),
                  full((1, nb))],
        out_specs=[pl.BlockSpec((L, hy_w), lambda o: (0, o))] * 2,
        out_shape=[jax.ShapeDtypeStruct((L, order * hy_w), F32)] * 2,
        compiler_params=_cparams(("arbitrary",)),
        name="hyena_filters",
    )(w1[0:1], w1[1:1 + nb], w1[1 + nb:], b1.reshape(1, hid), wh, bh.reshape(wh.shape[0], 1, hid),
      freq.reshape(1, hid), wout, decay.reshape(1, -1), bands)


def _dft_kernel(c32_ref, s32_ref, cb_ref, s1_ref, s2_ref, *, L, rb):
    i = pl.program_id(0)
    f = lax.broadcasted_iota(jnp.int32, (rb, L), 0) + i * rb
    t = lax.broadcasted_iota(jnp.int32, (rb, L), 1)
    k = (f * t) % (2 * L)
    ang = k.astype(F32) * (math.pi / L)
    c = jnp.cos(ang)
    s = -jnp.sin(ang)
    c32_ref[...] = c
    s32_ref[...] = s
    cb_ref[...] = c.astype(BF16)
    sign_t = jnp.where(t % 2 == 0, 1.0, -1.0)
    sign_f = jnp.where(f % 2 == 0, 1.0, -1.0)
    s1_ref[...] = jnp.where(f == 0, sign_t, s).astype(BF16)
    s2_ref[...] = jnp.where(t == 0, sign_f, s).astype(BF16)


def _dft_mats(L):
    rb = min(256, L)
    spec = pl.BlockSpec((rb, L), lambda i: (i, 0))
    return pl.pallas_call(
        functools.partial(_dft_kernel, L=L, rb=rb),
        grid=(L // rb,),
        in_specs=[],
        out_specs=[spec] * 5,
        out_shape=[jax.ShapeDtypeStruct((L, L), F32)] * 2 + [jax.ShapeDtypeStruct((L, L), BF16)] * 3,
        compiler_params=_cparams(("arbitrary",)),
        name="dft_mats",
    )()


def _spec_kernel(c_ref, s_ref, hs_ref, hd_ref, kr_ref, ki_ref, kn_ref, *, L, rb):
    i = pl.program_id(1)
    f = lax.broadcasted_iota(jnp.int32, (rb, 1), 0) + i * rb
    w = jnp.where(f == 0, 1.0, 2.0) * (1.0 / (2 * L))
    dot = functools.partial(jnp.dot, preferred_element_type=F32, precision=HIGHEST)
    hs = hs_ref[...]
    kr_ref[0] = w * dot(c_ref[...], hs)
    ki_ref[0] = w * dot(s_ref[...], hd_ref[...])
    sign = jnp.where(lax.broadcasted_iota(jnp.int32, (L, 1), 0) % 2 == 0, 1.0, -1.0)
    kn_ref[0] = jnp.sum(sign * hs, axis=0, keepdims=True) * (1.0 / (2 * L))


def _hy_spectra(c32, s32, hs, hd, hy_w, order):
    L = c32.shape[0]
    rb = min(256, L)
    mat = pl.BlockSpec((rb, L), lambda o, i: (i, 0))
    hsp = pl.BlockSpec((L, hy_w), lambda o, i: (0, o))
    return pl.pallas_call(
        functools.partial(_spec_kernel, L=L, rb=rb),
        grid=(order, L // rb),
        in_specs=[mat, mat, hsp, hsp],
        out_specs=[pl.BlockSpec((1, rb, hy_w), lambda o, i: (o, i, 0))] * 2
                  + [pl.BlockSpec((1, 1, hy_w), lambda o, i: (o, 0, 0))],
        out_shape=[jax.ShapeDtypeStruct((order, L, hy_w), F32)] * 2
                  + [jax.ShapeDtypeStruct((order, 1, hy_w), F32)],
        compiler_params=_cparams(("arbitrary", "arbitrary")),
        name="hyena_spectra",
    )(c32, s32, hs, hd)


def _hy_conv_kernel(v_ref, x1_ref, x2_ref, c_ref, s_ref, kr_ref, ki_ref, kn_ref, d_ref, o_ref,
                    zf, zb, pre, pim, *, rb, order):
    o = pl.program_id(1)
    ph = pl.program_id(2)
    blk = pl.program_id(3)
    rows = pl.ds(pl.multiple_of(blk * rb, rb), rb)

    @pl.when((o == 0) & (ph == 0) & (blk == 0))
    def _():
        v = v_ref[...]
        zf[...] = v
        zb[...] = v.astype(BF16)

    @pl.when(ph == 0)
    def _():
        z = zb[...]
        zr = jnp.dot(c_ref[rows, :], z, preferred_element_type=F32)
        zi = jnp.dot(s_ref[0, rows, :], z, preferred_element_type=F32)
        kr, ki = kr_ref[0], ki_ref[0]
        first = (lax.broadcasted_iota(jnp.int32, zr.shape, 0) == 0) & (blk == 0)
        pr = kr * zr - jnp.where(first, 0.0, ki * zi)
        pi = jnp.where(first, kn_ref[0] * zi, kr * zi + ki * zr)
        pre[rows, :] = pr.astype(BF16)
        pim[rows, :] = pi.astype(BF16)

    @pl.when(ph == 1)
    def _():
        y = (jnp.dot(c_ref[rows, :], pre[...], preferred_element_type=F32)
             + jnp.dot(s_ref[1, rows, :], pim[...], preferred_element_type=F32))
        zin = zf[rows, :]
        gate = jnp.where(o == 0, x1_ref[rows, :], x2_ref[rows, :])
        zn = gate * (y + d_ref[0] * zin)
        zf[rows, :] = zn
        zb[rows, :] = zn.astype(BF16)

        @pl.when(o == order - 1)
        def _():
            o_ref[rows, :] = zn


def _hy_conv(v, x1, x2, cb, s12, kr, ki, kn, d_hy, hy_w):
    L, BC = v.shape
    order = kr.shape[0]
    nc = 256
    rb = min(512, L)
    nblk = L // rb
    nct = hy_w // nc
    col = pl.BlockSpec((L, nc), lambda c, o, p, k: (0, c))
    kmap = lambda c, o, p, k: (o, k * (1 - p) + (nblk - 1) * p, c % nct)
    return pl.pallas_call(
        functools.partial(_hy_conv_kernel, rb=rb, order=order),
        grid=(BC // nc, order, 2, nblk),
        in_specs=[col, col, col,
                  pl.BlockSpec((L, L), lambda c, o, p, k: (0, 0), pipeline_mode=pl.Buffered(1)),
                  pl.BlockSpec((2, L, L), lambda c, o, p, k: (0, 0, 0), pipeline_mode=pl.Buffered(1)),
                  pl.BlockSpec((1, rb, nc), kmap),
                  pl.BlockSpec((1, rb, nc), kmap),
                  pl.BlockSpec((1, 1, nc), lambda c, o, p, k: (o, 0, c % nct)),
                  pl.BlockSpec((1, 1, nc), lambda c, o, p, k: (o, 0, c % nct))],
        out_specs=col,
        out_shape=jax.ShapeDtypeStruct((L, BC), F32),
        scratch_shapes=[pltpu.VMEM((L, nc), F32), pltpu.VMEM((L, nc), BF16),
                        pltpu.VMEM((L, nc), BF16), pltpu.VMEM((L, nc), BF16)],
        compiler_params=_cparams(("arbitrary",) * 4),
        name="hyena_conv",
    )(v, x1, x2, cb, s12, kr, ki, kn, d_hy.reshape(order, 1, hy_w))


def _merge_kernel(y_ref, z_ref, x_ref, mod_ref, wg_ref, bg_ref, g5_ref, gh_ref, wo5_ref, woh_ref, g2_ref,
                  x1_ref, h2t_ref):
    y = y_ref[0] + y_ref[1]
    gy = _gelu(y)
    gate = jnp.dot(gy.astype(BF16), wg_ref[...], preferred_element_type=F32) + bg_ref[...]
    s5 = gy * _sigmoid(gate)
    n5 = _rms(s5, g5_ref[...])
    nh = _rms(z_ref[...], gh_ref[...])
    mix = (jnp.dot(n5.astype(BF16), wo5_ref[...], preferred_element_type=F32)
           + jnp.dot(nh.astype(BF16), woh_ref[...], preferred_element_type=F32))
    x1 = x_ref[0] + mod_ref[0, 2:3, :] * mix
    x1_ref[0] = x1
    h2 = _rms(x1, g2_ref[...]) * (1.0 + mod_ref[0, 4:5, :]) + mod_ref[0, 3:4, :]
    h2t_ref[0] = h2.T.astype(BF16)


def _merge(y, z, x, mod3, w_glu, b_glu, g5, gh, w_out, g2):
    B, L, D = x.shape
    W5 = w_glu.shape[0]
    WH = z.shape[1] // B
    tl = min(512, L)
    full = lambda shp: pl.BlockSpec(shp, lambda b_, t: (0,) * len(shp))
    return pl.pallas_call(
        _merge_kernel,
        grid=(B, L // tl),
        in_specs=[pl.BlockSpec((2, tl, W5), lambda b_, t: (0, t, b_)),
                  pl.BlockSpec((tl, WH), lambda b_, t: (t, b_)),
                  pl.BlockSpec((1, tl, D), lambda b_, t: (b_, t, 0)),
                  pl.BlockSpec((1, N_MOD, D), lambda b_, t: (b_, 0, 0)),
                  full((W5, W5)), full((1, W5)), full((1, W5)), full((1, WH)),
                  full((W5, D)), full((WH, D)), full((1, D))],
        out_specs=[pl.BlockSpec((1, tl, D), lambda b_, t: (b_, t, 0)),
                   pl.BlockSpec((1, D, tl), lambda b_, t: (b_, 0, t))],
        out_shape=[jax.ShapeDtypeStruct((B, L, D), F32), jax.ShapeDtypeStruct((B, D, L), BF16)],
        compiler_params=_cparams(("arbitrary", "arbitrary")),
        name="merge_heads",
    )(y, z, x, mod3, w_glu.astype(BF16), b_glu.reshape(1, W5), g5.reshape(1, W5), gh.reshape(1, WH),
      w_out[:W5].astype(BF16), w_out[W5:].astype(BF16), g2.reshape(1, D))


def _topk_rank(s, k):
    vals = []
    cur = s
    rank = jnp.full(s.shape, float(k), F32)
    for i in range(k):
        m = jnp.max(cur, axis=0, keepdims=True)
        hit = cur == m
        vals.append(m)
        cur = jnp.where(hit, -jnp.inf, cur)
        rank = jnp.where(hit, float(i), rank)
    return jnp.concatenate(vals, axis=0), rank


def _topk_desc(s, k):
    vals = []
    cur = s
    for _ in range(k):
        m = jnp.max(cur, axis=0, keepdims=True)
        vals.append(m)
        cur = jnp.where(cur == m, -jnp.inf, cur)
    return jnp.concatenate(vals, axis=0)


def _gelu_packed(x):
    c = math.sqrt(2.0 / math.pi)
    inner = x * (c + (c * 0.044715) * (x * x))
    hx = 0.5 * x
    return hx + hx * jnp.tanh(inner)


PACK = 16


def _peer_kernel(h_ref, x1_ref, mod_ref, wq_ref, k1_ref, k2_ref, u_ref, vt_ref, gf_ref, o_ref,
                 ea, cs, eb, r2b, acc, a0, a1, g0, g1, *, nb1, topk):
    s = pl.program_id(1)
    ne = pl.num_programs(1) - 2
    H, NK, _ = k1_ref.shape
    T = h_ref.shape[2]

    @pl.when(s == 0)
    def _():
        acc[...] = jnp.zeros_like(acc)
        for buf in (a0, a1, g0, g1):
            buf[...] = jnp.zeros_like(buf)
        qt = jnp.dot(wq_ref[...], h_ref[0], preferred_element_type=F32)
        dk = k1_ref.shape[2]
        for hh in range(H):
            q1 = qt[(2 * hh) * dk:(2 * hh + 1) * dk, :]
            q2 = qt[(2 * hh + 1) * dk:(2 * hh + 2) * dk, :]
            s1 = jnp.dot(k1_ref[hh], q1, preferred_element_type=F32, precision=HIGHEST)
            s2 = jnp.dot(k2_ref[hh], q2, preferred_element_type=F32, precision=HIGHEST)
            v1, r1 = _topk_rank(s1, topk)
            v2, r2 = _topk_rank(s2, topk)
            cands = [v1[a:a + 1] + v2[b:b + 1]
                     for a in range(topk) for b in range(topk) if (a + 1) * (b + 1) <= topk]
            sel = _topk_desc(jnp.concatenate(cands, axis=0), topk)
            tau = sel[topk - 1:topk]
            mx = sel[0:1]
            rz = 1.0 / jnp.sum(jnp.exp(sel - mx), axis=0, keepdims=True)
            cnt = jnp.zeros((NK, T), F32)
            for a_ in range(topk):
                c_a = jnp.sum(jnp.where(v1[a_:a_ + 1] + v2 >= tau, 1.0, 0.0), axis=0, keepdims=True)
                cnt = jnp.where(r1 == float(a_), c_a, cnt)
            ea[hh] = jnp.exp(s1 - v1[0:1]) * rz
            cs[hh] = cnt
            eb[hh] = jnp.exp(s2 - v2[0:1]).astype(BF16)
            r2b[hh] = r2.astype(BF16)

    e_mid = jnp.clip(s - 1, 0, ne - 1)

    def stages(a_new, a_old, g_new, g_old):
        th = T // 2
        neb = nb1 * NK
        pieces = []
        for half in range(2):
            tc = slice(half * th, (half + 1) * th)
            for m0 in range(0, neb, MXU_ROWS):
                pieces.append(("a", slice(m0, m0 + MXU_ROWS), tc))
            for m0 in range(0, vt_ref.shape[0], MXU_ROWS):
                pieces.append(("acc", slice(m0, m0 + MXU_ROWS), tc))

        def mxu_piece(kind, ms, tc):
            if kind == "a":
                a_new[ms, tc] = jnp.dot(u_ref[ms, :], h_ref[0, :, tc], preferred_element_type=F32)
            else:
                acc[ms, tc] += jnp.dot(vt_ref[ms, :], g_old[:, tc], preferred_element_type=F32)

        row_cache = {}

        def packed_rows(j):
            if j not in row_cache:
                i1 = e_mid * nb1 + j
                row_cache[j] = tuple(
                    [jnp.broadcast_to(src[hh, pl.ds(i1, 1), :], (PACK, T)).astype(BF16) for hh in range(H)]
                    for src in (ea, cs))
            return row_cache[j]

        def vpu_chunk(j, r):
            ea_p, cs_p = packed_rows(j)
            rs = slice(r * PACK, (r + 1) * PACK)
            w = jnp.zeros((PACK, T), BF16)
            for hh in range(H):
                w = w + jnp.where(r2b[hh, rs, :] < cs_p[hh], ea_p[hh] * eb[hh, rs, :],
                                  jnp.zeros((), BF16))
            rows = slice(j * NK + r * PACK, j * NK + (r + 1) * PACK)
            g_new[rows, :] = _gelu_packed(a_old[rows, :].astype(BF16)) * w

        chunks = [(j, r) for j in range(nb1) for r in range(NK // PACK)]
        weight = [u_ref.shape[1] if kind == "a" else neb for kind, _, _ in pieces]
        done = 0
        for k, piece in enumerate(pieces):
            mxu_piece(*piece)
            upto = len(chunks) * sum(weight[:k + 1]) // sum(weight)
            for j, r in chunks[done:upto]:
                vpu_chunk(j, r)
            done = upto

    @pl.when(s % 2 == 0)
    def _():
        stages(a0, a1, g1, g0)

    @pl.when(s % 2 == 1)
    def _():
        stages(a1, a0, g0, g1)

    @pl.when(s == ne + 1)
    def _():
        x2 = x1_ref[0] + mod_ref[0, 5:6, :] * acc[...].T
        o_ref[0] = _rms(x2, gf_ref[...])


def _peer(h2t, x1, mod3, wq, k1, k2, u_tab, v_tab, g_final):
    B, L, D = x1.shape
    H, NK, dk = k1.shape
    T = min(512, L)
    nb1 = 4
    nlt = L // T
    ne = NK // nb1
    full = lambda shp: pl.BlockSpec(shp, lambda t, s: (0,) * len(shp))
    tok = pl.BlockSpec((1, T, D), lambda t, s: (t // nlt, t % nlt, 0))
    return pl.pallas_call(
        functools.partial(_peer_kernel, nb1=nb1, topk=PEER_TOPK),
        grid=(B * nlt, ne + 2),
        in_specs=[pl.BlockSpec((1, D, T), lambda t, s: (t // nlt, 0, t % nlt)), tok,
                  pl.BlockSpec((1, N_MOD, D), lambda t, s: (t // nlt, 0, 0)),
                  full(wq.shape[::-1]), full(k1.shape), full(k2.shape),
                  pl.BlockSpec((nb1 * NK, D), lambda t, s: (jnp.minimum(s, ne - 1), 0)),
                  pl.BlockSpec((D, nb1 * NK), lambda t, s: (0, jnp.clip(s - 2, 0, ne - 1))),
                  full((1, D))],
        out_specs=tok,
        out_shape=jax.ShapeDtypeStruct((B, L, D), F32),
        scratch_shapes=[pltpu.VMEM((H, NK, T), F32), pltpu.VMEM((H, NK, T), F32),
                        pltpu.VMEM((H, NK, T), BF16), pltpu.VMEM((H, NK, T), BF16),
                        pltpu.VMEM((D, T), F32),
                        pltpu.VMEM((nb1 * NK, T), F32), pltpu.VMEM((nb1 * NK, T), F32),
                        pltpu.VMEM((nb1 * NK, T), BF16), pltpu.VMEM((nb1 * NK, T), BF16)],
        compiler_params=_cparams(("arbitrary", "arbitrary")),
        name="peer_dense",
    )(h2t, x1, mod3, wq.astype(BF16).T, k1, k2, u_tab.astype(BF16), v_tab.astype(BF16).T, g_final.reshape(1, D))


def kernel(x, c, ctx, c_ctx, w_ada, b_ada, g_norm1, g_norm2, w_in, b_in, s5_a_re, s5_a_im, s5_log_step, s5_b_re, s5_b_im, s5_c_re, s5_c_im, s5_d, w_glu, b_glu, hy_conv_w, hy_conv_b, hf_w1, hf_b1, hf_wh, hf_bh, hf_freq, hf_wout, hf_decay, hy_d, g_out_s5, g_out_hy, w_out, peer_wq, peer_k1, peer_k2, peer_u, peer_v, g_final):
    B, L, D = x.shape
    Lc = ctx.shape[1]
    depth = w_ada.shape[0]
    assert depth == 1, "single-layer block"
    l = 0
    S5W = w_glu.shape[1]
    HYW = g_out_hy.shape[1]
    order = hy_d.shape[1]
    _, _, G, P, Hs = s5_b_re.shape

    rpad = (-(B + 1)) % 8
    cc = jnp.concatenate([c, c_ctx[None], jnp.zeros((rpad, D), F32)], axis=0)
    mod = _ada(cc, w_ada[l], b_ada[l])
    mod_x = mod[:B].reshape(B, N_MOD, D)
    mod_c = mod[B:B + 1].reshape(1, N_MOD, D)

    w_in_bf = w_in[l].astype(BF16)
    u_x, v, xg1, xg2 = _inproj(x, mod_x, g_norm1[l], w_in_bf, b_in[l], hy_conv_w[l], hy_conv_b[l],
                               s5_w=S5W, hy_w=HYW, hyena=True)
    (u_c,) = _inproj(ctx, mod_c, g_norm1[l], w_in_bf[:, :S5W], b_in[l][:S5W], None, None,
                     s5_w=S5W, hy_w=HYW, hyena=False)

    lr, li, bbr, bbi = _s5_prep(s5_a_re[l], s5_a_im[l], s5_log_step[l], s5_b_re[l], s5_b_im[l])
    gin = MXU_W // Hs
    lam = jnp.stack([lr.reshape(2, G * P), li.reshape(2, G * P)], axis=1)
    tr = lambda m: jnp.swapaxes(m, 2, 3)
    wb = jnp.stack([_block_diag(tr(bbr), gin), _block_diag(tr(bbi), gin)], axis=1).astype(BF16)
    cm = jnp.stack([_block_diag(tr(s5_c_re[l]), gin), _block_diag(-tr(s5_c_im[l]), gin)], axis=1).astype(BF16)
    h_zero = jnp.zeros((2, 2, B, G * P), F32)
    _, h_ctx = _s5_scan(u_c.reshape(Lc, B, S5W), h_zero, lam, wb, cm, s5_d[l], emit_y=False)
    y5, _ = _s5_scan(u_x.reshape(L, B, S5W), h_ctx, lam, wb, cm, s5_d[l], emit_y=True)

    hs, hd = _hy_filters(L, hf_w1[l], hf_b1[l], hf_wh[l], hf_bh[l], hf_freq[l], hf_wout[l], hf_decay[l],
                         HYW, order)
    c32, s32, cb, s1b, s2b = _dft_mats(L)
    kr, ki, kn = _hy_spectra(c32, s32, hs, hd, HYW, order)
    z_hy = _hy_conv(v, xg1, xg2, cb, jnp.stack([s1b, s2b]), kr, ki, kn, hy_d[l], HYW)

    x1, h2 = _merge(y5.reshape(2, L, B * S5W), z_hy, x, mod_x, w_glu[l], b_glu[l],
                    g_out_s5[l], g_out_hy[l], w_out[l], g_norm2[l])
    return _peer(h2, x1, mod_x, peer_wq[l], peer_k1[l], peer_k2[l], peer_u[l], peer_v[l], g_final)
```

```python
import functools
import math

import jax
import jax.numpy as jnp
from jax import lax
from jax.experimental import pallas as pl
from jax.experimental.pallas import tpu as pltpu

F32 = jnp.float32
BF16 = jnp.bfloat16
HIGHEST = lax.Precision.HIGHEST

EPS = 1e-6
GRID_W = 64
PEER_TOPK = 16
N_MOD = 6

MXU_W = 256
MXU_ROWS = 512
VPU_UNIT = 4
VMEM_LIMIT = 56 * 1024 * 1024


def _cparams(sem):
    return pltpu.CompilerParams(dimension_semantics=sem, vmem_limit_bytes=VMEM_LIMIT)


def _gelu(x):
    c = math.sqrt(2.0 / math.pi)
    return 0.5 * x * (1.0 + jnp.tanh(c * (x + 0.044715 * (x * x * x))))


def _sigmoid(x):
    return 1.0 / (1.0 + jnp.exp(-x))


def _rms(x, g):
    return x * lax.rsqrt(jnp.mean(x * x, axis=-1, keepdims=True) + EPS) * g


def _ada_kernel(c_ref, w_ref, b_ref, o_ref):
    c = c_ref[...]
    s = c * _sigmoid(c)
    o_ref[...] = jnp.dot(s, w_ref[...], preferred_element_type=F32, precision=HIGHEST) + b_ref[...]


def _ada(cc, w, b):
    R, D = cc.shape
    N = w.shape[1]
    tn = 1024
    return pl.pallas_call(
        _ada_kernel,
        grid=(N // tn,),
        in_specs=[pl.BlockSpec((R, D), lambda j: (0, 0)),
                  pl.BlockSpec((D, tn), lambda j: (0, j)),
                  pl.BlockSpec((1, tn), lambda j: (0, j))],
        out_specs=pl.BlockSpec((R, tn), lambda j: (0, j)),
        out_shape=jax.ShapeDtypeStruct((R, N), F32),
        compiler_params=_cparams(("arbitrary",)),
        name="ada_mod",
    )(cc, w, b.reshape(1, N))


def _inproj_kernel(x_ref, mod_ref, g_ref, w_ref, b_ref, *rest, s5_w, hy_w, hyena):
    x = x_ref[0]
    h = _rms(x, g_ref[...])
    h = h * (1.0 + mod_ref[0, 1:2, :]) + mod_ref[0, 0:1, :]
    p = jnp.dot(h.astype(BF16), w_ref[...], preferred_element_type=F32) + b_ref[...]
    if not hyena:
        (u_ref,) = rest
        u_ref[...] = p
        return
    cw_ref, cb_ref, u_ref, v_ref, x1_ref, x2_ref = rest
    u_ref[...] = p[:, :s5_w]
    ph = p[:, s5_w:]
    tl = ph.shape[0]
    pos = lax.broadcasted_iota(jnp.int32, ph.shape, 0) % GRID_W
    prev = jnp.where(pos == 0, 0.0, pltpu.roll(ph, 1, axis=0))
    nxt = jnp.where(pos == GRID_W - 1, 0.0, pltpu.roll(ph, tl - 1, axis=0))
    q = cb_ref[...] + cw_ref[0:1, :] * prev + cw_ref[1:2, :] * ph + cw_ref[2:3, :] * nxt
    v_ref[...] = q[:, :hy_w]
    x1_ref[...] = q[:, hy_w:2 * hy_w]
    x2_ref[...] = q[:, 2 * hy_w:]


def _inproj(x, mod3, g, w_bf, b, conv_w, conv_b, *, s5_w, hy_w, hyena):
    B, L, D = x.shape
    N = w_bf.shape[1]
    tl = min(512, L)
    nb = mod3.shape[0]
    mod_map = (lambda b_, t: (b_, 0, 0)) if nb > 1 else (lambda b_, t: (0, 0, 0))
    in_specs = [pl.BlockSpec((1, tl, D), lambda b_, t: (b_, t, 0)),
                pl.BlockSpec((1, N_MOD, D), mod_map),
                pl.BlockSpec((1, D), lambda b_, t: (0, 0)),
                pl.BlockSpec((D, N), lambda b_, t: (0, 0)),
                pl.BlockSpec((1, N), lambda b_, t: (0, 0))]
    args = [x, mod3, g.reshape(1, D), w_bf, b.reshape(1, N)]
    out_specs = [pl.BlockSpec((tl, s5_w), lambda b_, t: (t, b_))]
    out_shape = [jax.ShapeDtypeStruct((L, B * s5_w), F32)]
    if hyena:
        nh = N - s5_w
        in_specs += [pl.BlockSpec((3, nh), lambda b_, t: (0, 0)),
                     pl.BlockSpec((1, nh), lambda b_, t: (0, 0))]
        args += [conv_w, conv_b.reshape(1, nh)]
        out_specs += [pl.BlockSpec((tl, hy_w), lambda b_, t: (t, b_))] * 3
        out_shape += [jax.ShapeDtypeStruct((L, B * hy_w), F32)] * 3
    return pl.pallas_call(
        functools.partial(_inproj_kernel, s5_w=s5_w, hy_w=hy_w, hyena=hyena),
        grid=(B, L // tl),
        in_specs=in_specs, out_specs=out_specs, out_shape=out_shape,
        compiler_params=_cparams(("arbitrary", "arbitrary")),
        name="inproj_hy" if hyena else "inproj_ctx",
    )(*args)


def _s5_prep_kernel(are_ref, aim_ref, ls_ref, bre_ref, bim_ref, lr_ref, li_ref, bbr_ref, bbi_ref):
    a_re, a_im = are_ref[...], aim_ref[...]
    step = jnp.exp(ls_ref[...])
    er = jnp.exp(a_re * step)
    lr = er * jnp.cos(a_im * step)
    li = er * jnp.sin(a_im * step)
    nr, ni = lr - 1.0, li
    d2 = a_re * a_re + a_im * a_im
    kr = (nr * a_re + ni * a_im) / d2
    ki = (ni * a_re - nr * a_im) / d2
    br, bi = bre_ref[...], bim_ref[...]
    lr_ref[...] = lr
    li_ref[...] = li
    bbr_ref[...] = kr * br - ki * bi
    bbi_ref[...] = kr * bi + ki * br


def _s5_prep(a_re, a_im, log_step, b_re, b_im):
    two, G, P, H = b_re.shape
    n = two * G * P
    col = lambda a: a.reshape(n, 1)
    ls = jnp.broadcast_to(log_step[:, :, None], (two, G, P))
    outs = pl.pallas_call(
        _s5_prep_kernel,
        out_shape=[jax.ShapeDtypeStruct((n, 1), F32)] * 2 + [jax.ShapeDtypeStruct((n, H), F32)] * 2,
        name="s5_prep",
    )(col(a_re), col(a_im), col(ls), b_re.reshape(n, H), b_im.reshape(n, H))
    lr, li, bbr, bbi = outs
    return (lr.reshape(two, G, P), li.reshape(two, G, P),
            bbr.reshape(two, G, P, H), bbi.reshape(two, G, P, H))


def _block_diag(m, gb):
    two, G, R, C = m.shape
    m = m.reshape(two, G // gb, gb, R, C)
    eye = jnp.eye(gb, dtype=m.dtype)
    out = m[:, :, :, :, None, :] * eye[None, None, :, None, :, None]
    return out.reshape(two, G // gb, gb * R, gb * C)


def _s5_kernel(u_ref, h0_ref, lam_ref, wb_ref, cm_ref, d_ref, *rest, tl, nb, cw, emit_y):
    if emit_y:
        y_ref, hfin_ref, sre, sim, hre, him = rest
    else:
        hfin_ref, sre, sim, hre, him = rest
    dr = pl.program_id(0)
    i = pl.program_id(1)
    B = u_ref.shape[1]
    SW = sre.shape[1]
    nsb = wb_ref.shape[2]
    kin = wb_ref.shape[3]
    kst = wb_ref.shape[4]

    @pl.when(i == 0)
    def _():
        hre[...] = h0_ref[0, 0]
        him[...] = h0_ref[0, 1]

    u = u_ref[...].reshape(tl * B, u_ref.shape[2])
    ub = u.astype(BF16)
    for j in range(nsb):
        uj = ub[:, j * kin:(j + 1) * kin]
        sre[:, j * kst:(j + 1) * kst] = jnp.dot(uj, wb_ref[0, 0, j], preferred_element_type=F32)
        sim[:, j * kst:(j + 1) * kst] = jnp.dot(uj, wb_ref[0, 1, j], preferred_element_type=F32)

    for c in range(SW // cw):
        cols = slice(c * cw, (c + 1) * cw)
        lr = jnp.broadcast_to(lam_ref[0, 0:1, cols], (B, cw))
        li = jnp.broadcast_to(lam_ref[0, 1:2, cols], (B, cw))

        def body(s, carry, cols=cols, lr=lr, li=li):
            h_r, h_i = carry
            t = jnp.where(dr == 0, s, tl - 1 - s)
            row = pl.multiple_of(t * B, B)
            n_r = lr * h_r - li * h_i + sre[pl.ds(row, B), cols]
            n_i = lr * h_i + li * h_r + sim[pl.ds(row, B), cols]
            sre[pl.ds(row, B), cols] = n_r
            sim[pl.ds(row, B), cols] = n_i
            return n_r, n_i

        h_r, h_i = lax.fori_loop(0, tl, body, (hre[:, cols], him[:, cols]), unroll=2)
        hre[:, cols] = h_r
        him[:, cols] = h_i

    hfin_ref[0, 0] = hre[...]
    hfin_ref[0, 1] = him[...]

    if emit_y:
        kout = cm_ref.shape[4]
        dsel = d_ref[...] * jnp.where(dr == 0, 1.0, 0.0)
        for n in range(nsb):
            y = (jnp.dot(sre[:, n * kst:(n + 1) * kst].astype(BF16), cm_ref[0, 0, n], preferred_element_type=F32)
                 + jnp.dot(sim[:, n * kst:(n + 1) * kst].astype(BF16), cm_ref[0, 1, n], preferred_element_type=F32))
            oc = slice(n * kout, (n + 1) * kout)
            y = y + dsel[:, oc] * u[:, oc]
            y_ref[0, :, :, oc] = y.reshape(tl, B, kout)


def _s5_scan(u3, h0, lam, wb, cm, d, *, emit_y):
    L, B, W = u3.shape
    SW = lam.shape[2]
    tl = min(32, L)
    nT = L // tl
    tmap = lambda dr, i: i + dr * (nT - 1 - 2 * i)
    in_specs = [pl.BlockSpec((tl, B, W), lambda dr, i: (tmap(dr, i), 0, 0)),
                pl.BlockSpec((1, 2, B, SW), lambda dr, i: (dr, 0, 0, 0)),
                pl.BlockSpec((1, 2, SW), lambda dr, i: (dr, 0, 0)),
                pl.BlockSpec((1,) + wb.shape[1:], lambda dr, i: (dr, 0, 0, 0, 0)),
                pl.BlockSpec((1,) + cm.shape[1:], lambda dr, i: (dr, 0, 0, 0, 0)),
                pl.BlockSpec((1, W), lambda dr, i: (0, 0))]
    out_specs = [pl.BlockSpec((1, 2, B, SW), lambda dr, i: (dr, 0, 0, 0))]
    out_shape = [jax.ShapeDtypeStruct((2, 2, B, SW), F32)]
    if emit_y:
        out_specs = [pl.BlockSpec((1, tl, B, W), lambda dr, i: (dr, tmap(dr, i), 0, 0))] + out_specs
        out_shape = [jax.ShapeDtypeStruct((2, L, B, W), F32)] + out_shape
    res = pl.pallas_call(
        functools.partial(_s5_kernel, tl=tl, nb=nT, cw=256, emit_y=emit_y),
        grid=(2, nT),
        in_specs=in_specs, out_specs=out_specs, out_shape=out_shape,
        scratch_shapes=[pltpu.VMEM((tl * B, SW), F32), pltpu.VMEM((tl * B, SW), F32),
                        pltpu.VMEM((B, SW), F32), pltpu.VMEM((B, SW), F32)],
        compiler_params=_cparams(("arbitrary", "arbitrary")),
        name="s5_scan_x" if emit_y else "s5_scan_ctx",
    )(u3, h0, lam, wb, cm, d.reshape(1, W))
    return res if emit_y else (None, res[0])


def _hy_filter_kernel(w1t_ref, w1c_ref, w1s_ref, b1_ref, wh_ref, bh_ref, fr_ref, wout_ref, dec_ref,
                      bands_ref, hs_ref, hd_ref, *, L, hy_w):
    pos = lax.broadcasted_iota(jnp.int32, (L, 1), 0).astype(F32)
    t = pos / max(L - 1, 1)
    ang = (2.0 * math.pi / L) * pos * bands_ref[...]
    fr = fr_ref[...]
    dot = functools.partial(jnp.dot, preferred_element_type=F32, precision=HIGHEST)
    pre = t * w1t_ref[...] + dot(jnp.cos(ang), w1c_ref[...]) + dot(-jnp.sin(ang), w1s_ref[...]) + b1_ref[...]
    hid = jnp.sin(fr * pre)
    for i in range(wh_ref.shape[0]):
        hid = jnp.sin(fr * (dot(hid, wh_ref[i]) + bh_ref[i]))
    h = dot(hid, wout_ref[...]) * jnp.exp(-t * jnp.abs(dec_ref[...]))
    nrm = jnp.sum(h * h, axis=0, keepdims=True)
    nrm = nrm[:, :hy_w] + nrm[:, hy_w:]
    scale = lax.rsqrt(nrm + EPS)
    hf = h[:, :hy_w] * scale
    hb = h[:, hy_w:] * scale
    hb0 = jnp.where(lax.broadcasted_iota(jnp.int32, hb.shape, 0) == 0, 0.0, hb)
    hs_ref[...] = hf + hb0
    hd_ref[...] = hf - hb0


def _hy_filters(L, w1, b1, wh, bh, freq, wout, decay, hy_w, order):
    nb = (w1.shape[0] - 1) // 2
    hid = w1.shape[1]
    bands = jnp.linspace(1e-4, nb - 1, nb, dtype=F32).reshape(1, nb)
    full = lambda shp: pl.BlockSpec(shp, lambda o: (0,) * len(shp))
    return pl.pallas_call(
        functools.partial(_hy_filter_kernel, L=L, hy_w=hy_w),
        grid=(order,),
        in_specs=[full((1, hid)), full((nb, hid)), full((nb, hid)), full((1, hid)),
                  full(wh.shape), full((wh.shape[0], 1, hid)), full((1, hid)),
                  pl.BlockSpec((hid, 2 * hy_w), lambda o: (0, o)),
                  pl.BlockSpec((1, 2 * hy_w), lambda o: (0, o)),
                  full((1, nb))],
        out_specs=[pl.BlockSpec((L, hy_w), lambda o: (0, o))] * 2,
        out_shape=[jax.ShapeDtypeStruct((L, order * hy_w), F32)] * 2,
        compiler_params=_cparams(("arbitrary",)),
        name="hyena_filters",
    )(w1[0:1], w1[1:1 + nb], w1[1 + nb:], b1.reshape(1, hid), wh, bh.reshape(wh.shape[0], 1, hid),
      freq.reshape(1, hid), wout, decay.reshape(1, -1), bands)


def _dft_kernel(c32_ref, s32_ref, cb_ref, s1_ref, s2_ref, *, L, rb):
    i = pl.program_id(0)
    f = lax.broadcasted_iota(jnp.int32, (rb, L), 0) + i * rb
    t = lax.broadcasted_iota(jnp.int32, (rb, L), 1)
    k = (f * t) % (2 * L)
    ang = k.astype(F32) * (math.pi / L)
    c = jnp.cos(ang)
    s = -jnp.sin(ang)
    c32_ref[...] = c
    s32_ref[...] = s
    cb_ref[...] = c.astype(BF16)
    sign_t = jnp.where(t % 2 == 0, 1.0, -1.0)
    sign_f = jnp.where(f % 2 == 0, 1.0, -1.0)
    s1_ref[...] = jnp.where(f == 0, sign_t, s).astype(BF16)
    s2_ref[...] = jnp.where(t == 0, sign_f, s).astype(BF16)


def _dft_mats(L):
    rb = min(256, L)
    spec = pl.BlockSpec((rb, L), lambda i: (i, 0))
    return pl.pallas_call(
        functools.partial(_dft_kernel, L=L, rb=rb),
        grid=(L // rb,),
        in_specs=[],
        out_specs=[spec] * 5,
        out_shape=[jax.ShapeDtypeStruct((L, L), F32)] * 2 + [jax.ShapeDtypeStruct((L, L), BF16)] * 3,
        compiler_params=_cparams(("arbitrary",)),
        name="dft_mats",
    )()


def _spec_kernel(c_ref, s_ref, hs_ref, hd_ref, kr_ref, ki_ref, kn_ref, *, L, rb):
    i = pl.program_id(1)
    f = lax.broadcasted_iota(jnp.int32, (rb, 1), 0) + i * rb
    w = jnp.where(f == 0, 1.0, 2.0) * (1.0 / (2 * L))
    dot = functools.partial(jnp.dot, preferred_element_type=F32, precision=HIGHEST)
    hs = hs_ref[...]
    kr_ref[0] = w * dot(c_ref[...], hs)
    ki_ref[0] = w * dot(s_ref[...], hd_ref[...])
    sign = jnp.where(lax.broadcasted_iota(jnp.int32, (L, 1), 0) % 2 == 0, 1.0, -1.0)
    kn_ref[0] = jnp.sum(sign * hs, axis=0, keepdims=True) * (1.0 / (2 * L))


def _hy_spectra(c32, s32, hs, hd, hy_w, order):
    L = c32.shape[0]
    rb = min(256, L)
    mat = pl.BlockSpec((rb, L), lambda o, i: (i, 0))
    hsp = pl.BlockSpec((L, hy_w), lambda o, i: (0, o))
    return pl.pallas_call(
        functools.partial(_spec_kernel, L=L, rb=rb),
        grid=(order, L // rb),
        in_specs=[mat, mat, hsp, hsp],
        out_specs=[pl.BlockSpec((1, rb, hy_w), lambda o, i: (o, i, 0))] * 2
                  + [pl.BlockSpec((1, 1, hy_w), lambda o, i: (o, 0, 0))],
        out_shape=[jax.ShapeDtypeStruct((order, L, hy_w), F32)] * 2
                  + [jax.ShapeDtypeStruct((order, 1, hy_w), F32)],
        compiler_params=_cparams(("arbitrary", "arbitrary")),
        name="hyena_spectra",
    )(c32, s32, hs, hd)


def _hy_conv_kernel(v_ref, x1_ref, x2_ref, c_ref, s_ref, kr_ref, ki_ref, kn_ref, d_ref, o_ref,
                    zf, zb, pre, pim, *, rb, order):
    o = pl.program_id(1)
    ph = pl.program_id(2)
    blk = pl.program_id(3)
    rows = pl.ds(pl.multiple_of(blk * rb, rb), rb)

    @pl.when((o == 0) & (ph == 0) & (blk == 0))
    def _():
        v = v_ref[...]
        zf[...] = v
        zb[...] = v.astype(BF16)

    @pl.when(ph == 0)
    def _():
        z = zb[...]
        zr = jnp.dot(c_ref[rows, :], z, preferred_element_type=F32)
        zi = jnp.dot(s_ref[0, rows, :], z, preferred_element_type=F32)
        kr, ki = kr_ref[0], ki_ref[0]
        first = (lax.broadcasted_iota(jnp.int32, zr.shape, 0) == 0) & (blk == 0)
        pr = kr * zr - jnp.where(first, 0.0, ki * zi)
        pi = jnp.where(first, kn_ref[0] * zi, kr * zi + ki * zr)
        pre[rows, :] = pr.astype(BF16)
        pim[rows, :] = pi.astype(BF16)

    @pl.when(ph == 1)
    def _():
        y = (jnp.dot(c_ref[rows, :], pre[...], preferred_element_type=F32)
             + jnp.dot(s_ref[1, rows, :], pim[...], preferred_element_type=F32))
        zin = zf[rows, :]
        gate = jnp.where(o == 0, x1_ref[rows, :], x2_ref[rows, :])
        zn = gate * (y + d_ref[0] * zin)
        zf[rows, :] = zn
        zb[rows, :] = zn.astype(BF16)

        @pl.when(o == order - 1)
        def _():
            o_ref[rows, :] = zn


def _hy_conv(v, x1, x2, cb, s12, kr, ki, kn, d_hy, hy_w):
    L, BC = v.shape
    order = kr.shape[0]
    nc = 256
    rb = min(1024, L)
    nblk = L // rb
    nct = hy_w // nc
    col = pl.BlockSpec((L, nc), lambda c, o, p, k: (0, c))
    kmap = lambda c, o, p, k: (o, k * (1 - p) + (nblk - 1) * p, c % nct)
    return pl.pallas_call(
        functools.partial(_hy_conv_kernel, rb=rb, order=order),
        grid=(BC // nc, order, 2, nblk),
        in_specs=[col, col, col,
                  pl.BlockSpec((L, L), lambda c, o, p, k: (0, 0), pipeline_mode=pl.Buffered(1)),
                  pl.BlockSpec((2, L, L), lambda c, o, p, k: (0, 0, 0), pipeline_mode=pl.Buffered(1)),
                  pl.BlockSpec((1, rb, nc), kmap),
                  pl.BlockSpec((1, rb, nc), kmap),
                  pl.BlockSpec((1, 1, nc), lambda c, o, p, k: (o, 0, c % nct)),
                  pl.BlockSpec((1, 1, nc), lambda c, o, p, k: (o, 0, c % nct))],
        out_specs=col,
        out_shape=jax.ShapeDtypeStruct((L, BC), F32),
        scratch_shapes=[pltpu.VMEM((L, nc), F32), pltpu.VMEM((L, nc), BF16),
                        pltpu.VMEM((L, nc), BF16), pltpu.VMEM((L, nc), BF16)],
        compiler_params=_cparams(("arbitrary",) * 4),
        name="hyena_conv",
    )(v, x1, x2, cb, s12, kr, ki, kn, d_hy.reshape(order, 1, hy_w))


def _merge_kernel(y_ref, z_ref, x_ref, mod_ref, wg_ref, bg_ref, g5_ref, gh_ref, wo5_ref, woh_ref, g2_ref,
                  x1_ref, h2t_ref):
    y = y_ref[0] + y_ref[1]
    gy = _gelu(y)
    gate = jnp.dot(gy.astype(BF16), wg_ref[...], preferred_element_type=F32) + bg_ref[...]
    s5 = gy * _sigmoid(gate)
    n5 = _rms(s5, g5_ref[...])
    nh = _rms(z_ref[...], gh_ref[...])
    mix = (jnp.dot(n5.astype(BF16), wo5_ref[...], preferred_element_type=F32)
           + jnp.dot(nh.astype(BF16), woh_ref[...], preferred_element_type=F32))
    x1 = x_ref[0] + mod_ref[0, 2:3, :] * mix
    x1_ref[0] = x1
    h2 = _rms(x1, g2_ref[...]) * (1.0 + mod_ref[0, 4:5, :]) + mod_ref[0, 3:4, :]
    h2t_ref[0] = h2.T.astype(BF16)


def _merge(y, z, x, mod3, w_glu, b_glu, g5, gh, w_out, g2):
    B, L, D = x.shape
    W5 = w_glu.shape[0]
    WH = z.shape[1] // B
    tl = min(512, L)
    full = lambda shp: pl.BlockSpec(shp, lambda b_, t: (0,) * len(shp))
    return pl.pallas_call(
        _merge_kernel,
        grid=(B, L // tl),
        in_specs=[pl.BlockSpec((2, tl, W5), lambda b_, t: (0, t, b_)),
                  pl.BlockSpec((tl, WH), lambda b_, t: (t, b_)),
                  pl.BlockSpec((1, tl, D), lambda b_, t: (b_, t, 0)),
                  pl.BlockSpec((1, N_MOD, D), lambda b_, t: (b_, 0, 0)),
                  full((W5, W5)), full((1, W5)), full((1, W5)), full((1, WH)),
                  full((W5, D)), full((WH, D)), full((1, D))],
        out_specs=[pl.BlockSpec((1, tl, D), lambda b_, t: (b_, t, 0)),
                   pl.BlockSpec((1, D, tl), lambda b_, t: (b_, 0, t))],
        out_shape=[jax.ShapeDtypeStruct((B, L, D), F32), jax.ShapeDtypeStruct((B, D, L), BF16)],
        compiler_params=_cparams(("arbitrary", "arbitrary")),
        name="merge_heads",
    )(y, z, x, mod3, w_glu.astype(BF16), b_glu.reshape(1, W5), g5.reshape(1, W5), gh.reshape(1, WH),
      w_out[:W5].astype(BF16), w_out[W5:].astype(BF16), g2.reshape(1, D))


def _topk_rank(s, k):
    vals = []
    cur = s
    rank = jnp.full(s.shape, float(k), F32)
    for i in range(k):
        m = jnp.max(cur, axis=0, keepdims=True)
        hit = cur == m
        vals.append(m)
        cur = jnp.where(hit, -jnp.inf, cur)
        rank = jnp.where(hit, float(i), rank)
    return jnp.concatenate(vals, axis=0), rank


def _topk_desc(s, k):
    vals = []
    cur = s
    for _ in range(k):
        m = jnp.max(cur, axis=0, keepdims=True)
        vals.append(m)
        cur = jnp.where(cur == m, -jnp.inf, cur)
    return jnp.concatenate(vals, axis=0)


def _gelu_packed(x):
    c = math.sqrt(2.0 / math.pi)
    inner = x * (c + (c * 0.044715) * (x * x))
    hx = 0.5 * x
    return hx + hx * jnp.tanh(inner)


PACK = 16


def _peer_kernel(h_ref, x1_ref, mod_ref, wq_ref, k1_ref, k2_ref, u_ref, vt_ref, gf_ref, o_ref,
                 ea, cs, eb, r2b, acc, a0, a1, g0, g1, *, nb1, ne, topk):
    f = pl.program_id(0)
    total = pl.num_programs(0) - 2
    H, NK, _ = k1_ref.shape
    T = h_ref.shape[2]
    f2 = jnp.clip(f - 1, 0, total - 1)
    f3 = jnp.clip(f - 2, 0, total - 1)
    par2 = (f2 // ne) % 2
    par3 = (f3 // ne) % 2
    e_mid = f2 % ne

    @pl.when(f == 0)
    def _():
        for buf in (a0, a1, g0, g1):
            buf[...] = jnp.zeros_like(buf)

    @pl.when((f % ne == 0) & (f < total))
    def _():
        par = (f // ne) % 2
        acc[par] = jnp.zeros(acc.shape[1:], F32)
        qt = jnp.dot(wq_ref[...], h_ref[0], preferred_element_type=F32)
        dk = k1_ref.shape[2]
        for hh in range(H):
            q1 = qt[(2 * hh) * dk:(2 * hh + 1) * dk, :]
            q2 = qt[(2 * hh + 1) * dk:(2 * hh + 2) * dk, :]
            s1 = jnp.dot(k1_ref[hh], q1, preferred_element_type=F32, precision=HIGHEST)
            s2 = jnp.dot(k2_ref[hh], q2, preferred_element_type=F32, precision=HIGHEST)
            v1, r1 = _topk_rank(s1, topk)
            v2, r2 = _topk_rank(s2, topk)
            cands = [v1[a:a + 1] + v2[b:b + 1]
                     for a in range(topk) for b in range(topk) if (a + 1) * (b + 1) <= topk]
            sel = _topk_desc(jnp.concatenate(cands, axis=0), topk)
            tau = sel[topk - 1:topk]
            mx = sel[0:1]
            rz = 1.0 / jnp.sum(jnp.exp(sel - mx), axis=0, keepdims=True)
            cnt = jnp.zeros((NK, T), F32)
            for a_ in range(topk):
                c_a = jnp.sum(jnp.where(v1[a_:a_ + 1] + v2 >= tau, 1.0, 0.0), axis=0, keepdims=True)
                cnt = jnp.where(r1 == float(a_), c_a, cnt)
            ea[par, hh] = jnp.exp(s1 - v1[0:1]) * rz
            cs[par, hh] = cnt
            eb[par, hh] = jnp.exp(s2 - v2[0:1]).astype(BF16)
            r2b[par, hh] = r2.astype(BF16)

    def stages(a_new, a_old, g_new, g_old):
        th = T // 2
        neb = nb1 * NK
        pieces = []
        for half in range(2):
            tc = slice(half * th, (half + 1) * th)
            for m0 in range(0, neb, MXU_ROWS):
                pieces.append(("a", slice(m0, m0 + MXU_ROWS), tc))
            for m0 in range(0, vt_ref.shape[0], MXU_ROWS):
                pieces.append(("acc", slice(m0, m0 + MXU_ROWS), tc))

        def mxu_piece(kind, ms, tc):
            if kind == "a":
                a_new[ms, tc] = jnp.dot(u_ref[ms, :], h_ref[0, :, tc], preferred_element_type=F32)
            else:
                acc[par3, ms, tc] += jnp.dot(vt_ref[ms, :], g_old[:, tc], preferred_element_type=F32)

        def vpu_unit(j, r_lo, r_hi):
            i1 = e_mid * nb1 + j
            rss = [slice(r * PACK, (r + 1) * PACK) for r in range(r_lo, r_hi)]
            ws = [jnp.zeros((PACK, T), BF16) for _ in rss]
            for hh in range(H):
                ea_p = jnp.broadcast_to(ea[par2, hh, pl.ds(i1, 1), :], (PACK, T)).astype(BF16)
                cs_p = jnp.broadcast_to(cs[par2, hh, pl.ds(i1, 1), :], (PACK, T)).astype(BF16)
                for k, rs in enumerate(rss):
                    ws[k] = ws[k] + jnp.where(r2b[par2, hh, rs, :] < cs_p, ea_p * eb[par2, hh, rs, :],
                                              jnp.zeros((), BF16))
            for k, rs in enumerate(rss):
                rows = slice(j * NK + rs.start, j * NK + rs.stop)
                g_new[rows, :] = _gelu_packed(a_old[rows, :].astype(BF16)) * ws[k]

        nr = NK // PACK
        units = [(j, r0, r0 + VPU_UNIT) for j in range(nb1) for r0 in range(0, nr, VPU_UNIT)]
        weight = [u_ref.shape[1] if kind == "a" else neb for kind, _, _ in pieces]
        done = 0
        for k, piece in enumerate(pieces):
            mxu_piece(*piece)
            upto = len(units) * sum(weight[:k + 1]) // sum(weight)
            for unit in units[done:upto]:
                vpu_unit(*unit)
            done = upto

    @pl.when(f % 2 == 0)
    def _():
        stages(a0, a1, g1, g0)

    @pl.when(f % 2 == 1)
    def _():
        stages(a1, a0, g0, g1)

    @pl.when((f >= 2) & ((f - 1) % ne == 0))
    def _():
        x2 = x1_ref[0] + mod_ref[0, 5:6, :] * acc[par3].T
        o_ref[0] = _rms(x2, gf_ref[...])


def _peer(h2t, x1, mod3, wq, k1, k2, u_tab, v_tab, g_final):
    B, L, D = x1.shape
    H, NK, dk = k1.shape
    T = min(512, L)
    nb1 = 8
    nlt = L // T
    ne = NK // nb1
    total = B * nlt * ne
    t_in = lambda f: jnp.minimum(f // ne, B * nlt - 1)
    t_out = lambda f: jnp.maximum(f - 2, 0) // ne
    once = lambda shp: pl.BlockSpec(shp, lambda f: (0,) * len(shp), pipeline_mode=pl.Buffered(1))
    tok = pl.BlockSpec((1, T, D), lambda f: (t_out(f) // nlt, t_out(f) % nlt, 0))
    return pl.pallas_call(
        functools.partial(_peer_kernel, nb1=nb1, ne=ne, topk=PEER_TOPK),
        grid=(total + 2,),
        in_specs=[pl.BlockSpec((1, D, T), lambda f: (t_in(f) // nlt, 0, t_in(f) % nlt)), tok,
                  pl.BlockSpec((1, N_MOD, D), lambda f: (t_out(f) // nlt, 0, 0)),
                  once(wq.shape[::-1]), once(k1.shape), once(k2.shape),
                  pl.BlockSpec((nb1 * NK, D), lambda f: (jnp.minimum(f, total - 1) % ne, 0)),
                  pl.BlockSpec((D, nb1 * NK), lambda f: (0, jnp.maximum(f - 2, 0) % ne)),
                  once((1, D))],
        out_specs=tok,
        out_shape=jax.ShapeDtypeStruct((B, L, D), F32),
        scratch_shapes=[pltpu.VMEM((2, H, NK, T), F32), pltpu.VMEM((2, H, NK, T), F32),
                        pltpu.VMEM((2, H, NK, T), BF16), pltpu.VMEM((2, H, NK, T), BF16),
                        pltpu.VMEM((2, D, T), F32),
                        pltpu.VMEM((nb1 * NK, T), F32), pltpu.VMEM((nb1 * NK, T), F32),
                        pltpu.VMEM((nb1 * NK, T), BF16), pltpu.VMEM((nb1 * NK, T), BF16)],
        compiler_params=_cparams(("arbitrary",)),
        name="peer_dense",
    )(h2t, x1, mod3, wq.astype(BF16).T, k1, k2, u_tab.astype(BF16), v_tab.astype(BF16).T, g_final.reshape(1, D))


def kernel(x, c, ctx, c_ctx, w_ada, b_ada, g_norm1, g_norm2, w_in, b_in, s5_a_re, s5_a_im, s5_log_step, s5_b_re, s5_b_im, s5_c_re, s5_c_im, s5_d, w_glu, b_glu, hy_conv_w, hy_conv_b, hf_w1, hf_b1, hf_wh, hf_bh, hf_freq, hf_wout, hf_decay, hy_d, g_out_s5, g_out_hy, w_out, peer_wq, peer_k1, peer_k2, peer_u, peer_v, g_final):
    B, L, D = x.shape
    Lc = ctx.shape[1]
    depth = w_ada.shape[0]
    assert depth == 1, "single-layer block"
    l = 0
    S5W = w_glu.shape[1]
    HYW = g_out_hy.shape[1]
    order = hy_d.shape[1]
    _, _, G, P, Hs = s5_b_re.shape

    rpad = (-(B + 1)) % 8
    cc = jnp.concatenate([c, c_ctx[None], jnp.zeros((rpad, D), F32)], axis=0)
    mod = _ada(cc, w_ada[l], b_ada[l])
    mod_x = mod[:B].reshape(B, N_MOD, D)
    mod_c = mod[B:B + 1].reshape(1, N_MOD, D)

    w_in_bf = w_in[l].astype(BF16)
    u_x, v, xg1, xg2 = _inproj(x, mod_x, g_norm1[l], w_in_bf, b_in[l], hy_conv_w[l], hy_conv_b[l],
                               s5_w=S5W, hy_w=HYW, hyena=True)
    (u_c,) = _inproj(ctx, mod_c, g_norm1[l], w_in_bf[:, :S5W], b_in[l][:S5W], None, None,
                     s5_w=S5W, hy_w=HYW, hyena=False)

    lr, li, bbr, bbi = _s5_prep(s5_a_re[l], s5_a_im[l], s5_log_step[l], s5_b_re[l], s5_b_im[l])
    gin = MXU_W // Hs
    lam = jnp.stack([lr.reshape(2, G * P), li.reshape(2, G * P)], axis=1)
    tr = lambda m: jnp.swapaxes(m, 2, 3)
    wb = jnp.stack([_block_diag(tr(bbr), gin), _block_diag(tr(bbi), gin)], axis=1).astype(BF16)
    cm = jnp.stack([_block_diag(tr(s5_c_re[l]), gin), _block_diag(-tr(s5_c_im[l]), gin)], axis=1).astype(BF16)
    h_zero = jnp.zeros((2, 2, B, G * P), F32)
    _, h_ctx = _s5_scan(u_c.reshape(Lc, B, S5W), h_zero, lam, wb, cm, s5_d[l], emit_y=False)
    y5, _ = _s5_scan(u_x.reshape(L, B, S5W), h_ctx, lam, wb, cm, s5_d[l], emit_y=True)

    hs, hd = _hy_filters(L, hf_w1[l], hf_b1[l], hf_wh[l], hf_bh[l], hf_freq[l], hf_wout[l], hf_decay[l],
                         HYW, order)
    c32, s32, cb, s1b, s2b = _dft_mats(L)
    kr, ki, kn = _hy_spectra(c32, s32, hs, hd, HYW, order)
    z_hy = _hy_conv(v, xg1, xg2, cb, jnp.stack([s1b, s2b]), kr, ki, kn, hy_d[l], HYW)

    x1, h2 = _merge(y5.reshape(2, L, B * S5W), z_hy, x, mod_x, w_glu[l], b_glu[l],
                    g_out_s5[l], g_out_hy[l], w_out[l], g_norm2[l])
    return _peer(h2, x1, mod_x, peer_wq[l], peer_k1[l], peer_k2[l], peer_u[l], peer_v[l], g_final)
```

```python
import functools
import math

import jax
import jax.numpy as jnp
from jax import lax
from jax.experimental import pallas as pl
from jax.experimental.pallas import tpu as pltpu

F32 = jnp.float32
BF16 = jnp.bfloat16
HIGHEST = lax.Precision.HIGHEST

EPS = 1e-6
GRID_W = 64
PEER_TOPK = 16
N_MOD = 6

MXU_W = 256
MXU_ROWS = 512
VPU_UNIT = 8
VMEM_LIMIT = 56 * 1024 * 1024


def _cparams(sem):
    return pltpu.CompilerParams(dimension_semantics=sem, vmem_limit_bytes=VMEM_LIMIT)


def _gelu(x):
    c = math.sqrt(2.0 / math.pi)
    return 0.5 * x * (1.0 + jnp.tanh(c * (x + 0.044715 * (x * x * x))))


def _sigmoid(x):
    return 1.0 / (1.0 + jnp.exp(-x))


def _rms(x, g):
    return x * lax.rsqrt(jnp.mean(x * x, axis=-1, keepdims=True) + EPS) * g


def _ada_kernel(c_ref, w_ref, b_ref, o_ref):
    c = c_ref[...]
    s = c * _sigmoid(c)
    o_ref[...] = jnp.dot(s, w_ref[...], preferred_element_type=F32, precision=HIGHEST) + b_ref[...]


def _ada(cc, w, b):
    R, D = cc.shape
    N = w.shape[1]
    tn = 1024
    return pl.pallas_call(
        _ada_kernel,
        grid=(N // tn,),
        in_specs=[pl.BlockSpec((R, D), lambda j: (0, 0)),
                  pl.BlockSpec((D, tn), lambda j: (0, j)),
                  pl.BlockSpec((1, tn), lambda j: (0, j))],
        out_specs=pl.BlockSpec((R, tn), lambda j: (0, j)),
        out_shape=jax.ShapeDtypeStruct((R, N), F32),
        compiler_params=_cparams(("arbitrary",)),
        name="ada_mod",
    )(cc, w, b.reshape(1, N))


def _inproj_kernel(x_ref, mod_ref, g_ref, w_ref, b_ref, *rest, s5_w, hy_w, hyena):
    x = x_ref[0]
    h = _rms(x, g_ref[...])
    h = h * (1.0 + mod_ref[0, 1:2, :]) + mod_ref[0, 0:1, :]
    p = jnp.dot(h.astype(BF16), w_ref[...], preferred_element_type=F32) + b_ref[...]
    if not hyena:
        (u_ref,) = rest
        u_ref[...] = p
        return
    cw_ref, cb_ref, u_ref, v_ref, x1_ref, x2_ref = rest
    u_ref[...] = p[:, :s5_w]
    ph = p[:, s5_w:]
    tl = ph.shape[0]
    pos = lax.broadcasted_iota(jnp.int32, ph.shape, 0) % GRID_W
    prev = jnp.where(pos == 0, 0.0, pltpu.roll(ph, 1, axis=0))
    nxt = jnp.where(pos == GRID_W - 1, 0.0, pltpu.roll(ph, tl - 1, axis=0))
    q = cb_ref[...] + cw_ref[0:1, :] * prev + cw_ref[1:2, :] * ph + cw_ref[2:3, :] * nxt
    v_ref[...] = q[:, :hy_w]
    x1_ref[...] = q[:, hy_w:2 * hy_w]
    x2_ref[...] = q[:, 2 * hy_w:]


def _inproj(x, mod3, g, w_bf, b, conv_w, conv_b, *, s5_w, hy_w, hyena):
    B, L, D = x.shape
    N = w_bf.shape[1]
    tl = min(512, L)
    nb = mod3.shape[0]
    mod_map = (lambda b_, t: (b_, 0, 0)) if nb > 1 else (lambda b_, t: (0, 0, 0))
    in_specs = [pl.BlockSpec((1, tl, D), lambda b_, t: (b_, t, 0)),
                pl.BlockSpec((1, N_MOD, D), mod_map),
                pl.BlockSpec((1, D), lambda b_, t: (0, 0)),
                pl.BlockSpec((D, N), lambda b_, t: (0, 0)),
                pl.BlockSpec((1, N), lambda b_, t: (0, 0))]
    args = [x, mod3, g.reshape(1, D), w_bf, b.reshape(1, N)]
    out_specs = [pl.BlockSpec((tl, s5_w), lambda b_, t: (t, b_))]
    out_shape = [jax.ShapeDtypeStruct((L, B * s5_w), F32)]
    if hyena:
        nh = N - s5_w
        in_specs += [pl.BlockSpec((3, nh), lambda b_, t: (0, 0)),
                     pl.BlockSpec((1, nh), lambda b_, t: (0, 0))]
        args += [conv_w, conv_b.reshape(1, nh)]
        out_specs += [pl.BlockSpec((tl, hy_w), lambda b_, t: (t, b_))] * 3
        out_shape += [jax.ShapeDtypeStruct((L, B * hy_w), F32)] * 3
    return pl.pallas_call(
        functools.partial(_inproj_kernel, s5_w=s5_w, hy_w=hy_w, hyena=hyena),
        grid=(B, L // tl),
        in_specs=in_specs, out_specs=out_specs, out_shape=out_shape,
        compiler_params=_cparams(("arbitrary", "arbitrary")),
        name="inproj_hy" if hyena else "inproj_ctx",
    )(*args)


def _s5_prep_kernel(are_ref, aim_ref, ls_ref, bre_ref, bim_ref, lr_ref, li_ref, bbr_ref, bbi_ref):
    a_re, a_im = are_ref[...], aim_ref[...]
    step = jnp.exp(ls_ref[...])
    er = jnp.exp(a_re * step)
    lr = er * jnp.cos(a_im * step)
    li = er * jnp.sin(a_im * step)
    nr, ni = lr - 1.0, li
    d2 = a_re * a_re + a_im * a_im
    kr = (nr * a_re + ni * a_im) / d2
    ki = (ni * a_re - nr * a_im) / d2
    br, bi = bre_ref[...], bim_ref[...]
    lr_ref[...] = lr
    li_ref[...] = li
    bbr_ref[...] = kr * br - ki * bi
    bbi_ref[...] = kr * bi + ki * br


def _s5_prep(a_re, a_im, log_step, b_re, b_im):
    two, G, P, H = b_re.shape
    n = two * G * P
    col = lambda a: a.reshape(n, 1)
    ls = jnp.broadcast_to(log_step[:, :, None], (two, G, P))
    outs = pl.pallas_call(
        _s5_prep_kernel,
        out_shape=[jax.ShapeDtypeStruct((n, 1), F32)] * 2 + [jax.ShapeDtypeStruct((n, H), F32)] * 2,
        name="s5_prep",
    )(col(a_re), col(a_im), col(ls), b_re.reshape(n, H), b_im.reshape(n, H))
    lr, li, bbr, bbi = outs
    return (lr.reshape(two, G, P), li.reshape(two, G, P),
            bbr.reshape(two, G, P, H), bbi.reshape(two, G, P, H))


def _block_diag(m, gb):
    two, G, R, C = m.shape
    m = m.reshape(two, G // gb, gb, R, C)
    eye = jnp.eye(gb, dtype=m.dtype)
    out = m[:, :, :, :, None, :] * eye[None, None, :, None, :, None]
    return out.reshape(two, G // gb, gb * R, gb * C)


def _s5_kernel(u_ref, h0_ref, lam_ref, wb_ref, cm_ref, d_ref, *rest, tl, nb, cw, emit_y):
    if emit_y:
        y_ref, hfin_ref, sre, sim, hre, him = rest
    else:
        hfin_ref, sre, sim, hre, him = rest
    dr = pl.program_id(0)
    i = pl.program_id(1)
    B = u_ref.shape[1]
    SW = sre.shape[1]
    nsb = wb_ref.shape[2]
    kin = wb_ref.shape[3]
    kst = wb_ref.shape[4]

    @pl.when(i == 0)
    def _():
        hre[...] = h0_ref[0, 0]
        him[...] = h0_ref[0, 1]

    u = u_ref[...].reshape(tl * B, u_ref.shape[2])
    ub = u.astype(BF16)
    for j in range(nsb):
        uj = ub[:, j * kin:(j + 1) * kin]
        sre[:, j * kst:(j + 1) * kst] = jnp.dot(uj, wb_ref[0, 0, j], preferred_element_type=F32)
        sim[:, j * kst:(j + 1) * kst] = jnp.dot(uj, wb_ref[0, 1, j], preferred_element_type=F32)

    for c in range(SW // cw):
        cols = slice(c * cw, (c + 1) * cw)
        lr = jnp.broadcast_to(lam_ref[0, 0:1, cols], (B, cw))
        li = jnp.broadcast_to(lam_ref[0, 1:2, cols], (B, cw))

        def body(s, carry, cols=cols, lr=lr, li=li):
            h_r, h_i = carry
            t = jnp.where(dr == 0, s, tl - 1 - s)
            row = pl.multiple_of(t * B, B)
            n_r = lr * h_r - li * h_i + sre[pl.ds(row, B), cols]
            n_i = lr * h_i + li * h_r + sim[pl.ds(row, B), cols]
            sre[pl.ds(row, B), cols] = n_r
            sim[pl.ds(row, B), cols] = n_i
            return n_r, n_i

        h_r, h_i = lax.fori_loop(0, tl, body, (hre[:, cols], him[:, cols]), unroll=2)
        hre[:, cols] = h_r
        him[:, cols] = h_i

    hfin_ref[0, 0] = hre[...]
    hfin_ref[0, 1] = him[...]

    if emit_y:
        kout = cm_ref.shape[4]
        dsel = d_ref[...] * jnp.where(dr == 0, 1.0, 0.0)
        for n in range(nsb):
            y = (jnp.dot(sre[:, n * kst:(n + 1) * kst].astype(BF16), cm_ref[0, 0, n], preferred_element_type=F32)
                 + jnp.dot(sim[:, n * kst:(n + 1) * kst].astype(BF16), cm_ref[0, 1, n], preferred_element_type=F32))
            oc = slice(n * kout, (n + 1) * kout)
            y = y + dsel[:, oc] * u[:, oc]
            y_ref[0, :, :, oc] = y.reshape(tl, B, kout)


def _s5_scan(u3, h0, lam, wb, cm, d, *, emit_y):
    L, B, W = u3.shape
    SW = lam.shape[2]
    tl = min(32, L)
    nT = L // tl
    tmap = lambda dr, i: i + dr * (nT - 1 - 2 * i)
    in_specs = [pl.BlockSpec((tl, B, W), lambda dr, i: (tmap(dr, i), 0, 0)),
                pl.BlockSpec((1, 2, B, SW), lambda dr, i: (dr, 0, 0, 0)),
                pl.BlockSpec((1, 2, SW), lambda dr, i: (dr, 0, 0)),
                pl.BlockSpec((1,) + wb.shape[1:], lambda dr, i: (dr, 0, 0, 0, 0)),
                pl.BlockSpec((1,) + cm.shape[1:], lambda dr, i: (dr, 0, 0, 0, 0)),
                pl.BlockSpec((1, W), lambda dr, i: (0, 0))]
    out_specs = [pl.BlockSpec((1, 2, B, SW), lambda dr, i: (dr, 0, 0, 0))]
    out_shape = [jax.ShapeDtypeStruct((2, 2, B, SW), F32)]
    if emit_y:
        out_specs = [pl.BlockSpec((1, tl, B, W), lambda dr, i: (dr, tmap(dr, i), 0, 0))] + out_specs
        out_shape = [jax.ShapeDtypeStruct((2, L, B, W), F32)] + out_shape
    res = pl.pallas_call(
        functools.partial(_s5_kernel, tl=tl, nb=nT, cw=256, emit_y=emit_y),
        grid=(2, nT),
        in_specs=in_specs, out_specs=out_specs, out_shape=out_shape,
        scratch_shapes=[pltpu.VMEM((tl * B, SW), F32), pltpu.VMEM((tl * B, SW), F32),
                        pltpu.VMEM((B, SW), F32), pltpu.VMEM((B, SW), F32)],
        compiler_params=_cparams(("arbitrary", "arbitrary")),
        name="s5_scan_x" if emit_y else "s5_scan_ctx",
    )(u3, h0, lam, wb, cm, d.reshape(1, W))
    return res if emit_y else (None, res[0])


def _hy_filter_kernel(w1t_ref, w1c_ref, w1s_ref, b1_ref, wh_ref, bh_ref, fr_ref, wout_ref, dec_ref,
                      bands_ref, hs_ref, hd_ref, *, L, hy_w):
    pos = lax.broadcasted_iota(jnp.int32, (L, 1), 0).astype(F32)
    t = pos / max(L - 1, 1)
    ang = (2.0 * math.pi / L) * pos * bands_ref[...]
    fr = fr_ref[...]
    dot = functools.partial(jnp.dot, preferred_element_type=F32, precision=HIGHEST)
    pre = t * w1t_ref[...] + dot(jnp.cos(ang), w1c_ref[...]) + dot(-jnp.sin(ang), w1s_ref[...]) + b1_ref[...]
    hid = jnp.sin(fr * pre)
    for i in range(wh_ref.shape[0]):
        hid = jnp.sin(fr * (dot(hid, wh_ref[i]) + bh_ref[i]))
    h = dot(hid, wout_ref[...]) * jnp.exp(-t * jnp.abs(dec_ref[...]))
    nrm = jnp.sum(h * h, axis=0, keepdims=True)
    nrm = nrm[:, :hy_w] + nrm[:, hy_w:]
    scale = lax.rsqrt(nrm + EPS)
    hf = h[:, :hy_w] * scale
    hb = h[:, hy_w:] * scale
    hb0 = jnp.where(lax.broadcasted_iota(jnp.int32, hb.shape, 0) == 0, 0.0, hb)
    hs_ref[...] = hf + hb0
    hd_ref[...] = hf - hb0


def _hy_filters(L, w1, b1, wh, bh, freq, wout, decay, hy_w, order):
    nb = (w1.shape[0] - 1) // 2
    hid = w1.shape[1]
    bands = jnp.linspace(1e-4, nb - 1, nb, dtype=F32).reshape(1, nb)
    full = lambda shp: pl.BlockSpec(shp, lambda o: (0,) * len(shp))
    return pl.pallas_call(
        functools.partial(_hy_filter_kernel, L=L, hy_w=hy_w),
        grid=(order,),
        in_specs=[full((1, hid)), full((nb, hid)), full((nb, hid)), full((1, hid)),
                  full(wh.shape), full((wh.shape[0], 1, hid)), full((1, hid)),
                  pl.BlockSpec((hid, 2 * hy_w), lambda o: (0, o)),
                  pl.BlockSpec((1, 2 * hy_w), lambda o: (0, o)),
                  full((1, nb))],
        out_specs=[pl.BlockSpec((L, hy_w), lambda o: (0, o))] * 2,
        out_shape=[jax.ShapeDtypeStruct((L, order * hy_w), F32)] * 2,
        compiler_params=_cparams(("arbitrary",)),
        name="hyena_filters",
    )(w1[0:1], w1[1:1 + nb], w1[1 + nb:], b1.reshape(1, hid), wh, bh.reshape(wh.shape[0], 1, hid),
      freq.reshape(1, hid), wout, decay.reshape(1, -1), bands)


def _dft_kernel(c32_ref, s32_ref, cb_ref, s1_ref, s2_ref, *, L, rb):
    i = pl.program_id(0)
    f = lax.broadcasted_iota(jnp.int32, (rb, L), 0) + i * rb
    t = lax.broadcasted_iota(jnp.int32, (rb, L), 1)
    k = (f * t) % (2 * L)
    ang = k.astype(F32) * (math.pi / L)
    c = jnp.cos(ang)
    s = -jnp.sin(ang)
    c32_ref[...] = c
    s32_ref[...] = s
    cb_ref[...] = c.astype(BF16)
    sign_t = jnp.where(t % 2 == 0, 1.0, -1.0)
    sign_f = jnp.where(f % 2 == 0, 1.0, -1.0)
    s1_ref[...] = jnp.where(f == 0, sign_t, s).astype(BF16)
    s2_ref[...] = jnp.where(t == 0, sign_f, s).astype(BF16)


def _dft_mats(L):
    rb = min(256, L)
    spec = pl.BlockSpec((rb, L), lambda i: (i, 0))
    return pl.pallas_call(
        functools.partial(_dft_kernel, L=L, rb=rb),
        grid=(L // rb,),
        in_specs=[],
        out_specs=[spec] * 5,
        out_shape=[jax.ShapeDtypeStruct((L, L), F32)] * 2 + [jax.ShapeDtypeStruct((L, L), BF16)] * 3,
        compiler_params=_cparams(("arbitrary",)),
        name="dft_mats",
    )()


def _spec_kernel(c_ref, s_ref, hs_ref, hd_ref, kr_ref, ki_ref, kn_ref, *, L, rb):
    i = pl.program_id(1)
    f = lax.broadcasted_iota(jnp.int32, (rb, 1), 0) + i * rb
    w = jnp.where(f == 0, 1.0, 2.0) * (1.0 / (2 * L))
    dot = functools.partial(jnp.dot, preferred_element_type=F32, precision=HIGHEST)
    hs = hs_ref[...]
    kr_ref[0] = w * dot(c_ref[...], hs)
    ki_ref[0] = w * dot(s_ref[...], hd_ref[...])
    sign = jnp.where(lax.broadcasted_iota(jnp.int32, (L, 1), 0) % 2 == 0, 1.0, -1.0)
    kn_ref[0] = jnp.sum(sign * hs, axis=0, keepdims=True) * (1.0 / (2 * L))


def _hy_spectra(c32, s32, hs, hd, hy_w, order):
    L = c32.shape[0]
    rb = min(256, L)
    mat = pl.BlockSpec((rb, L), lambda o, i: (i, 0))
    hsp = pl.BlockSpec((L, hy_w), lambda o, i: (0, o))
    return pl.pallas_call(
        functools.partial(_spec_kernel, L=L, rb=rb),
        grid=(order, L // rb),
        in_specs=[mat, mat, hsp, hsp],
        out_specs=[pl.BlockSpec((1, rb, hy_w), lambda o, i: (o, i, 0))] * 2
                  + [pl.BlockSpec((1, 1, hy_w), lambda o, i: (o, 0, 0))],
        out_shape=[jax.ShapeDtypeStruct((order, L, hy_w), F32)] * 2
                  + [jax.ShapeDtypeStruct((order, 1, hy_w), F32)],
        compiler_params=_cparams(("arbitrary", "arbitrary")),
        name="hyena_spectra",
    )(c32, s32, hs, hd)


def _hy_conv_kernel(v_ref, x1_ref, x2_ref, c_ref, s_ref, kr_ref, ki_ref, kn_ref, d_ref, o_ref,
                    zf, zb, pre, pim, *, rb, order):
    o = pl.program_id(1)
    ph = pl.program_id(2)
    blk = pl.program_id(3)
    rows = pl.ds(pl.multiple_of(blk * rb, rb), rb)

    @pl.when((o == 0) & (ph == 0) & (blk == 0))
    def _():
        v = v_ref[...]
        zf[...] = v
        zb[...] = v.astype(BF16)

    @pl.when(ph == 0)
    def _():
        z = zb[...]
        zr = jnp.dot(c_ref[rows, :], z, preferred_element_type=F32)
        zi = jnp.dot(s_ref[0, rows, :], z, preferred_element_type=F32)
        kr, ki = kr_ref[0], ki_ref[0]
        first = (lax.broadcasted_iota(jnp.int32, zr.shape, 0) == 0) & (blk == 0)
        pr = kr * zr - jnp.where(first, 0.0, ki * zi)
        pi = jnp.where(first, kn_ref[0] * zi, kr * zi + ki * zr)
        pre[rows, :] = pr.astype(BF16)
        pim[rows, :] = pi.astype(BF16)

    @pl.when(ph == 1)
    def _():
        y = (jnp.dot(c_ref[rows, :], pre[...], preferred_element_type=F32)
             + jnp.dot(s_ref[1, rows, :], pim[...], preferred_element_type=F32))
        zin = zf[rows, :]
        gate = jnp.where(o == 0, x1_ref[rows, :], x2_ref[rows, :])
        zn = gate * (y + d_ref[0] * zin)
        zf[rows, :] = zn
        zb[rows, :] = zn.astype(BF16)

        @pl.when(o == order - 1)
        def _():
            o_ref[rows, :] = zn


def _hy_conv(v, x1, x2, cb, s12, kr, ki, kn, d_hy, hy_w):
    L, BC = v.shape
    order = kr.shape[0]
    nc = 256
    rb = min(1024, L)
    nblk = L // rb
    nct = hy_w // nc
    col = pl.BlockSpec((L, nc), lambda c, o, p, k: (0, c))
    kmap = lambda c, o, p, k: (o, k * (1 - p) + (nblk - 1) * p, c % nct)
    return pl.pallas_call(
        functools.partial(_hy_conv_kernel, rb=rb, order=order),
        grid=(BC // nc, order, 2, nblk),
        in_specs=[col, col, col,
                  pl.BlockSpec((L, L), lambda c, o, p, k: (0, 0), pipeline_mode=pl.Buffered(1)),
                  pl.BlockSpec((2, L, L), lambda c, o, p, k: (0, 0, 0), pipeline_mode=pl.Buffered(1)),
                  pl.BlockSpec((1, rb, nc), kmap),
                  pl.BlockSpec((1, rb, nc), kmap),
                  pl.BlockSpec((1, 1, nc), lambda c, o, p, k: (o, 0, c % nct)),
                  pl.BlockSpec((1, 1, nc), lambda c, o, p, k: (o, 0, c % nct))],
        out_specs=col,
        out_shape=jax.ShapeDtypeStruct((L, BC), F32),
        scratch_shapes=[pltpu.VMEM((L, nc), F32), pltpu.VMEM((L, nc), BF16),
                        pltpu.VMEM((L, nc), BF16), pltpu.VMEM((L, nc), BF16)],
        compiler_params=_cparams(("arbitrary",) * 4),
        name="hyena_conv",
    )(v, x1, x2, cb, s12, kr, ki, kn, d_hy.reshape(order, 1, hy_w))


def _merge_kernel(y_ref, z_ref, x_ref, mod_ref, wg_ref, bg_ref, g5_ref, gh_ref, wo5_ref, woh_ref, g2_ref,
                  x1_ref, h2t_ref):
    y = y_ref[0] + y_ref[1]
    gy = _gelu(y)
    gate = jnp.dot(gy.astype(BF16), wg_ref[...], preferred_element_type=F32) + bg_ref[...]
    s5 = gy * _sigmoid(gate)
    n5 = _rms(s5, g5_ref[...])
    nh = _rms(z_ref[...], gh_ref[...])
    mix = (jnp.dot(n5.astype(BF16), wo5_ref[...], preferred_element_type=F32)
           + jnp.dot(nh.astype(BF16), woh_ref[...], preferred_element_type=F32))
    x1 = x_ref[0] + mod_ref[0, 2:3, :] * mix
    x1_ref[0] = x1
    h2 = _rms(x1, g2_ref[...]) * (1.0 + mod_ref[0, 4:5, :]) + mod_ref[0, 3:4, :]
    h2t_ref[0] = h2.T.astype(BF16)


def _merge(y, z, x, mod3, w_glu, b_glu, g5, gh, w_out, g2):
    B, L, D = x.shape
    W5 = w_glu.shape[0]
    WH = z.shape[1] // B
    tl = min(512, L)
    full = lambda shp: pl.BlockSpec(shp, lambda b_, t: (0,) * len(shp))
    return pl.pallas_call(
        _merge_kernel,
        grid=(B, L // tl),
        in_specs=[pl.BlockSpec((2, tl, W5), lambda b_, t: (0, t, b_)),
                  pl.BlockSpec((tl, WH), lambda b_, t: (t, b_)),
                  pl.BlockSpec((1, tl, D), lambda b_, t: (b_, t, 0)),
                  pl.BlockSpec((1, N_MOD, D), lambda b_, t: (b_, 0, 0)),
                  full((W5, W5)), full((1, W5)), full((1, W5)), full((1, WH)),
                  full((W5, D)), full((WH, D)), full((1, D))],
        out_specs=[pl.BlockSpec((1, tl, D), lambda b_, t: (b_, t, 0)),
                   pl.BlockSpec((1, D, tl), lambda b_, t: (b_, 0, t))],
        out_shape=[jax.ShapeDtypeStruct((B, L, D), F32), jax.ShapeDtypeStruct((B, D, L), BF16)],
        compiler_params=_cparams(("arbitrary", "arbitrary")),
        name="merge_heads",
    )(y, z, x, mod3, w_glu.astype(BF16), b_glu.reshape(1, W5), g5.reshape(1, W5), gh.reshape(1, WH),
      w_out[:W5].astype(BF16), w_out[W5:].astype(BF16), g2.reshape(1, D))


def _topk_rank(s, k):
    vals = []
    cur = s
    rank = jnp.full(s.shape, float(k), F32)
    for i in range(k):
        m = jnp.max(cur, axis=0, keepdims=True)
        hit = cur == m
        vals.append(m)
        cur = jnp.where(hit, -jnp.inf, cur)
        rank = jnp.where(hit, float(i), rank)
    return jnp.concatenate(vals, axis=0), rank


def _topk_desc(s, k):
    vals = []
    cur = s
    for _ in range(k):
        m = jnp.max(cur, axis=0, keepdims=True)
        vals.append(m)
        cur = jnp.where(cur == m, -jnp.inf, cur)
    return jnp.concatenate(vals, axis=0)


def _gelu_packed(x):
    c = math.sqrt(2.0 / math.pi)
    inner = x * (c + (c * 0.044715) * (x * x))
    hx = 0.5 * x
    return hx + hx * jnp.tanh(inner)


PACK = 16
LANES = 128
SUBLANES = 8


def _packed_row(row):
    return jnp.broadcast_to(row, (PACK, row.shape[1])).astype(BF16)


def _peer_kernel(h_ref, x1_ref, mod_ref, wq_ref, k1_ref, k2_ref, u_ref, vt_ref, gf_ref, o_ref,
                 ea, cs, eb, r2b, acc, a0, a1, g0, g1, *, nb1, ne, topk):
    f = pl.program_id(0)
    total = pl.num_programs(0) - 2
    H, NK, _ = k1_ref.shape
    T = h_ref.shape[2]
    f2 = jnp.clip(f - 1, 0, total - 1)
    f3 = jnp.clip(f - 2, 0, total - 1)
    par2 = (f2 // ne) % 2
    par3 = (f3 // ne) % 2
    e_mid = f2 % ne

    @pl.when(f == 0)
    def _():
        for buf in (a0, a1, g0, g1):
            buf[...] = jnp.zeros_like(buf)

    @pl.when((f % ne == 0) & (f < total))
    def _():
        par = (f // ne) % 2
        acc[par] = jnp.zeros(acc.shape[1:], F32)
        qt = jnp.dot(wq_ref[...], h_ref[0], preferred_element_type=F32)
        dk = k1_ref.shape[2]
        for hh in range(H):
            q1 = qt[(2 * hh) * dk:(2 * hh + 1) * dk, :]
            q2 = qt[(2 * hh + 1) * dk:(2 * hh + 2) * dk, :]
            s1f = jnp.dot(k1_ref[hh], q1, preferred_element_type=F32, precision=HIGHEST)
            s2f = jnp.dot(k2_ref[hh], q2, preferred_element_type=F32, precision=HIGHEST)
            for c0 in range(0, T, LANES):
                lc = slice(c0, c0 + LANES)
                s1, s2 = s1f[:, lc], s2f[:, lc]
                v1 = _topk_desc(s1, topk)
                v2, r2 = _topk_rank(s2, topk)
                cands = [v1[a:a + 1] + v2[b:b + 1]
                         for a in range(topk) for b in range(topk) if (a + 1) * (b + 1) <= topk]
                sel = _topk_desc(jnp.concatenate(cands, axis=0), topk)
                tau = sel[topk - 1:topk]
                mx = sel[0:1]
                rz = 1.0 / jnp.sum(jnp.exp(sel - mx), axis=0, keepdims=True)
                cnt = jnp.zeros((NK, LANES), F32)
                for a_ in range(topk):
                    c_a = jnp.sum(jnp.where(v1[a_:a_ + 1] + v2 >= tau, 1.0, 0.0), axis=0, keepdims=True)
                    cnt = jnp.where(s1 == v1[a_:a_ + 1], c_a, cnt)
                ea[par, hh, :, lc] = jnp.exp(s1 - v1[0:1]) * rz
                cs[par, hh, :, lc] = cnt
                eb[par, hh, :, lc] = jnp.exp(s2 - v2[0:1]).astype(BF16)
                r2b[par, hh, :, lc] = r2.astype(BF16)

    def stages(a_new, a_old, g_new, g_old):
        th = T // 2
        neb = nb1 * NK
        pieces = []
        for half in range(2):
            tc = slice(half * th, (half + 1) * th)
            for m0 in range(0, neb, MXU_ROWS):
                pieces.append(("a", slice(m0, m0 + MXU_ROWS), tc))
            for m0 in range(0, vt_ref.shape[0], MXU_ROWS):
                pieces.append(("acc", slice(m0, m0 + MXU_ROWS), tc))

        def mxu_piece(kind, ms, tc):
            if kind == "a":
                a_new[ms, tc] = jnp.dot(u_ref[ms, :], h_ref[0, :, tc], preferred_element_type=F32)
            else:
                acc[par3, ms, tc] += jnp.dot(vt_ref[ms, :], g_old[:, tc], preferred_element_type=F32)

        def vpu_unit(j, r_lo, r_hi):
            i1 = e_mid * nb1 + j
            rss = [slice(r * PACK, (r + 1) * PACK) for r in range(r_lo, r_hi)]
            ws = [jnp.zeros((PACK, T), BF16) for _ in rss]
            for hh in range(H):
                ea_p = _packed_row(ea[par2, hh, pl.ds(i1, 1), :])
                cs_p = _packed_row(cs[par2, hh, pl.ds(i1, 1), :])
                for k, rs in enumerate(rss):
                    ws[k] = ws[k] + jnp.where(r2b[par2, hh, rs, :] < cs_p, ea_p * eb[par2, hh, rs, :],
                                              jnp.zeros((), BF16))
            for k, rs in enumerate(rss):
                rows = slice(j * NK + rs.start, j * NK + rs.stop)
                g_new[rows, :] = _gelu_packed(a_old[rows, :].astype(BF16)) * ws[k]

        nr = NK // PACK
        units = [(j, r0, r0 + VPU_UNIT) for j in range(nb1) for r0 in range(0, nr, VPU_UNIT)]
        weight = [u_ref.shape[1] if kind == "a" else neb for kind, _, _ in pieces]
        done = 0
        for k, piece in enumerate(pieces):
            mxu_piece(*piece)
            upto = len(units) * sum(weight[:k + 1]) // sum(weight)
            for unit in units[done:upto]:
                vpu_unit(*unit)
            done = upto

    @pl.when(f % 2 == 0)
    def _():
        stages(a0, a1, g1, g0)

    @pl.when(f % 2 == 1)
    def _():
        stages(a1, a0, g0, g1)

    @pl.when((f >= 2) & ((f - 1) % ne == 0))
    def _():
        x2 = x1_ref[0] + mod_ref[0, 5:6, :] * acc[par3].T
        o_ref[0] = _rms(x2, gf_ref[...])


def _peer(h2t, x1, mod3, wq, k1, k2, u_tab, v_tab, g_final):
    B, L, D = x1.shape
    H, NK, dk = k1.shape
    T = min(512, L)
    nb1 = 8
    nlt = L // T
    ne = NK // nb1
    total = B * nlt * ne
    t_in = lambda f: jnp.minimum(f // ne, B * nlt - 1)
    t_out = lambda f: jnp.maximum(f - 2, 0) // ne
    once = lambda shp: pl.BlockSpec(shp, lambda f: (0,) * len(shp), pipeline_mode=pl.Buffered(1))
    tok = pl.BlockSpec((1, T, D), lambda f: (t_out(f) // nlt, t_out(f) % nlt, 0))
    return pl.pallas_call(
        functools.partial(_peer_kernel, nb1=nb1, ne=ne, topk=PEER_TOPK),
        grid=(total + 2,),
        in_specs=[pl.BlockSpec((1, D, T), lambda f: (t_in(f) // nlt, 0, t_in(f) % nlt)), tok,
                  pl.BlockSpec((1, N_MOD, D), lambda f: (t_out(f) // nlt, 0, 0)),
                  once(wq.shape[::-1]), once(k1.shape), once(k2.shape),
                  pl.BlockSpec((nb1 * NK, D), lambda f: (jnp.minimum(f, total - 1) % ne, 0)),
                  pl.BlockSpec((D, nb1 * NK), lambda f: (0, jnp.maximum(f - 2, 0) % ne)),
                  once((1, D))],
        out_specs=tok,
        out_shape=jax.ShapeDtypeStruct((B, L, D), F32),
        scratch_shapes=[pltpu.VMEM((2, H, NK, T), F32), pltpu.VMEM((2, H, NK, T), F32),
                        pltpu.VMEM((2, H, NK, T), BF16), pltpu.VMEM((2, H, NK, T), BF16),
                        pltpu.VMEM((2, D, T), F32),
                        pltpu.VMEM((nb1 * NK, T), F32), pltpu.VMEM((nb1 * NK, T), F32),
                        pltpu.VMEM((nb1 * NK, T), BF16), pltpu.VMEM((nb1 * NK, T), BF16)],
        compiler_params=_cparams(("arbitrary",)),
        name="peer_dense",
    )(h2t, x1, mod3, wq.astype(BF16).T, k1, k2, u_tab.astype(BF16), v_tab.astype(BF16).T, g_final.reshape(1, D))


def kernel(x, c, ctx, c_ctx, w_ada, b_ada, g_norm1, g_norm2, w_in, b_in, s5_a_re, s5_a_im, s5_log_step, s5_b_re, s5_b_im, s5_c_re, s5_c_im, s5_d, w_glu, b_glu, hy_conv_w, hy_conv_b, hf_w1, hf_b1, hf_wh, hf_bh, hf_freq, hf_wout, hf_decay, hy_d, g_out_s5, g_out_hy, w_out, peer_wq, peer_k1, peer_k2, peer_u, peer_v, g_final):
    B, L, D = x.shape
    Lc = ctx.shape[1]
    depth = w_ada.shape[0]
    assert depth == 1, "single-layer block"
    l = 0
    S5W = w_glu.shape[1]
    HYW = g_out_hy.shape[1]
    order = hy_d.shape[1]
    _, _, G, P, Hs = s5_b_re.shape

    rpad = (-(B + 1)) % 8
    cc = jnp.concatenate([c, c_ctx[None], jnp.zeros((rpad, D), F32)], axis=0)
    mod = _ada(cc, w_ada[l], b_ada[l])
    mod_x = mod[:B].reshape(B, N_MOD, D)
    mod_c = mod[B:B + 1].reshape(1, N_MOD, D)

    w_in_bf = w_in[l].astype(BF16)
    u_x, v, xg1, xg2 = _inproj(x, mod_x, g_norm1[l], w_in_bf, b_in[l], hy_conv_w[l], hy_conv_b[l],
                               s5_w=S5W, hy_w=HYW, hyena=True)
    (u_c,) = _inproj(ctx, mod_c, g_norm1[l], w_in_bf[:, :S5W], b_in[l][:S5W], None, None,
                     s5_w=S5W, hy_w=HYW, hyena=False)

    lr, li, bbr, bbi = _s5_prep(s5_a_re[l], s5_a_im[l], s5_log_step[l], s5_b_re[l], s5_b_im[l])
    gin = MXU_W // Hs
    lam = jnp.stack([lr.reshape(2, G * P), li.reshape(2, G * P)], axis=1)
    tr = lambda m: jnp.swapaxes(m, 2, 3)
    wb = jnp.stack([_block_diag(tr(bbr), gin), _block_diag(tr(bbi), gin)], axis=1).astype(BF16)
    cm = jnp.stack([_block_diag(tr(s5_c_re[l]), gin), _block_diag(-tr(s5_c_im[l]), gin)], axis=1).astype(BF16)
    h_zero = jnp.zeros((2, 2, B, G * P), F32)
    _, h_ctx = _s5_scan(u_c.reshape(Lc, B, S5W), h_zero, lam, wb, cm, s5_d[l], emit_y=False)
    y5, _ = _s5_scan(u_x.reshape(L, B, S5W), h_ctx, lam, wb, cm, s5_d[l], emit_y=True)

    hs, hd = _hy_filters(L, hf_w1[l], hf_b1[l], hf_wh[l], hf_bh[l], hf_freq[l], hf_wout[l], hf_decay[l],
                         HYW, order)
    c32, s32, cb, s1b, s2b = _dft_mats(L)
    kr, ki, kn = _hy_spectra(c32, s32, hs, hd, HYW, order)
    z_hy = _hy_conv(v, xg1, xg2, cb, jnp.stack([s1b, s2b]), kr, ki, kn, hy_d[l], HYW)

    x1, h2 = _merge(y5.reshape(2, L, B * S5W), z_hy, x, mod_x, w_glu[l], b_glu[l],
                    g_out_s5[l], g_out_hy[l], w_out[l], g_norm2[l])
    return _peer(h2, x1, mod_x, peer_wq[l], peer_k1[l], peer_k2[l], peer_u[l], peer_v[l], g_final)
```

```python
import functools
import math

import jax
import jax.numpy as jnp
from jax import lax
from jax.experimental import pallas as pl
from jax.experimental.pallas import tpu as pltpu

F32 = jnp.float32
BF16 = jnp.bfloat16
HIGHEST = lax.Precision.HIGHEST

EPS = 1e-6
GRID_W = 64
PEER_TOPK = 16
N_MOD = 6

MXU_W = 256
MXU_ROWS = 512
VPU_UNIT = 8
VMEM_LIMIT = 56 * 1024 * 1024


def _cparams(sem):
    return pltpu.CompilerParams(dimension_semantics=sem, vmem_limit_bytes=VMEM_LIMIT)


def _gelu(x):
    c = math.sqrt(2.0 / math.pi)
    return 0.5 * x * (1.0 + jnp.tanh(c * (x + 0.044715 * (x * x * x))))


def _sigmoid(x):
    return 1.0 / (1.0 + jnp.exp(-x))


def _rms(x, g):
    return x * lax.rsqrt(jnp.mean(x * x, axis=-1, keepdims=True) + EPS) * g


def _ada_kernel(c_ref, w_ref, b_ref, o_ref):
    c = c_ref[...]
    s = c * _sigmoid(c)
    o_ref[...] = jnp.dot(s, w_ref[...], preferred_element_type=F32, precision=HIGHEST) + b_ref[...]


def _ada(cc, w, b):
    R, D = cc.shape
    N = w.shape[1]
    tn = 1024
    return pl.pallas_call(
        _ada_kernel,
        grid=(N // tn,),
        in_specs=[pl.BlockSpec((R, D), lambda j: (0, 0)),
                  pl.BlockSpec((D, tn), lambda j: (0, j)),
                  pl.BlockSpec((1, tn), lambda j: (0, j))],
        out_specs=pl.BlockSpec((R, tn), lambda j: (0, j)),
        out_shape=jax.ShapeDtypeStruct((R, N), F32),
        compiler_params=_cparams(("arbitrary",)),
        name="ada_mod",
    )(cc, w, b.reshape(1, N))


def _inproj_kernel(x_ref, mod_ref, g_ref, w_ref, b_ref, *rest, s5_w, hy_w, hyena):
    x = x_ref[0]
    h = _rms(x, g_ref[...])
    h = h * (1.0 + mod_ref[0, 1:2, :]) + mod_ref[0, 0:1, :]
    p = jnp.dot(h.astype(BF16), w_ref[...], preferred_element_type=F32) + b_ref[...]
    if not hyena:
        (u_ref,) = rest
        u_ref[...] = p
        return
    cw_ref, cb_ref, u_ref, v_ref, x1_ref, x2_ref = rest
    u_ref[...] = p[:, :s5_w]
    ph = p[:, s5_w:]
    tl = ph.shape[0]
    pos = lax.broadcasted_iota(jnp.int32, ph.shape, 0) % GRID_W
    prev = jnp.where(pos == 0, 0.0, pltpu.roll(ph, 1, axis=0))
    nxt = jnp.where(pos == GRID_W - 1, 0.0, pltpu.roll(ph, tl - 1, axis=0))
    q = cb_ref[...] + cw_ref[0:1, :] * prev + cw_ref[1:2, :] * ph + cw_ref[2:3, :] * nxt
    v_ref[...] = q[:, :hy_w]
    x1_ref[...] = q[:, hy_w:2 * hy_w]
    x2_ref[...] = q[:, 2 * hy_w:]


def _inproj(x, mod3, g, w_bf, b, conv_w, conv_b, *, s5_w, hy_w, hyena):
    B, L, D = x.shape
    N = w_bf.shape[1]
    tl = min(512, L)
    nb = mod3.shape[0]
    mod_map = (lambda b_, t: (b_, 0, 0)) if nb > 1 else (lambda b_, t: (0, 0, 0))
    in_specs = [pl.BlockSpec((1, tl, D), lambda b_, t: (b_, t, 0)),
                pl.BlockSpec((1, N_MOD, D), mod_map),
                pl.BlockSpec((1, D), lambda b_, t: (0, 0)),
                pl.BlockSpec((D, N), lambda b_, t: (0, 0)),
                pl.BlockSpec((1, N), lambda b_, t: (0, 0))]
    args = [x, mod3, g.reshape(1, D), w_bf, b.reshape(1, N)]
    out_specs = [pl.BlockSpec((tl, s5_w), lambda b_, t: (t, b_))]
    out_shape = [jax.ShapeDtypeStruct((L, B * s5_w), F32)]
    if hyena:
        nh = N - s5_w
        in_specs += [pl.BlockSpec((3, nh), lambda b_, t: (0, 0)),
                     pl.BlockSpec((1, nh), lambda b_, t: (0, 0))]
        args += [conv_w, conv_b.reshape(1, nh)]
        out_specs += [pl.BlockSpec((tl, hy_w), lambda b_, t: (t, b_))] * 3
        out_shape += [jax.ShapeDtypeStruct((L, B * hy_w), F32)] * 3
    return pl.pallas_call(
        functools.partial(_inproj_kernel, s5_w=s5_w, hy_w=hy_w, hyena=hyena),
        grid=(B, L // tl),
        in_specs=in_specs, out_specs=out_specs, out_shape=out_shape,
        compiler_params=_cparams(("arbitrary", "arbitrary")),
        name="inproj_hy" if hyena else "inproj_ctx",
    )(*args)


def _s5_prep_kernel(are_ref, aim_ref, ls_ref, bre_ref, bim_ref, lr_ref, li_ref, bbr_ref, bbi_ref):
    a_re, a_im = are_ref[...], aim_ref[...]
    step = jnp.exp(ls_ref[...])
    er = jnp.exp(a_re * step)
    lr = er * jnp.cos(a_im * step)
    li = er * jnp.sin(a_im * step)
    nr, ni = lr - 1.0, li
    d2 = a_re * a_re + a_im * a_im
    kr = (nr * a_re + ni * a_im) / d2
    ki = (ni * a_re - nr * a_im) / d2
    br, bi = bre_ref[...], bim_ref[...]
    lr_ref[...] = lr
    li_ref[...] = li
    bbr_ref[...] = kr * br - ki * bi
    bbi_ref[...] = kr * bi + ki * br


def _s5_prep(a_re, a_im, log_step, b_re, b_im):
    two, G, P, H = b_re.shape
    n = two * G * P
    col = lambda a: a.reshape(n, 1)
    ls = jnp.broadcast_to(log_step[:, :, None], (two, G, P))
    outs = pl.pallas_call(
        _s5_prep_kernel,
        out_shape=[jax.ShapeDtypeStruct((n, 1), F32)] * 2 + [jax.ShapeDtypeStruct((n, H), F32)] * 2,
        name="s5_prep",
    )(col(a_re), col(a_im), col(ls), b_re.reshape(n, H), b_im.reshape(n, H))
    lr, li, bbr, bbi = outs
    return (lr.reshape(two, G, P), li.reshape(two, G, P),
            bbr.reshape(two, G, P, H), bbi.reshape(two, G, P, H))


def _block_diag(m, gb):
    two, G, R, C = m.shape
    m = m.reshape(two, G // gb, gb, R, C)
    eye = jnp.eye(gb, dtype=m.dtype)
    out = m[:, :, :, :, None, :] * eye[None, None, :, None, :, None]
    return out.reshape(two, G // gb, gb * R, gb * C)


def _s5_kernel(u_ref, h0_ref, lam_ref, wb_ref, cm_ref, d_ref, *rest, tl, nb, cw, emit_y):
    if emit_y:
        y_ref, hfin_ref, sre, sim, hre, him = rest
    else:
        hfin_ref, sre, sim, hre, him = rest
    dr = pl.program_id(0)
    i = pl.program_id(1)
    B = u_ref.shape[1]
    SW = sre.shape[1]
    nsb = wb_ref.shape[2]
    kin = wb_ref.shape[3]
    kst = wb_ref.shape[4]

    @pl.when(i == 0)
    def _():
        hre[...] = h0_ref[0, 0]
        him[...] = h0_ref[0, 1]

    u = u_ref[...].reshape(tl * B, u_ref.shape[2])
    ub = u.astype(BF16)
    for j in range(nsb):
        uj = ub[:, j * kin:(j + 1) * kin]
        sre[:, j * kst:(j + 1) * kst] = jnp.dot(uj, wb_ref[0, 0, j], preferred_element_type=F32)
        sim[:, j * kst:(j + 1) * kst] = jnp.dot(uj, wb_ref[0, 1, j], preferred_element_type=F32)

    for c in range(SW // cw):
        cols = slice(c * cw, (c + 1) * cw)
        lr = jnp.broadcast_to(lam_ref[0, 0:1, cols], (B, cw))
        li = jnp.broadcast_to(lam_ref[0, 1:2, cols], (B, cw))

        def body(s, carry, cols=cols, lr=lr, li=li):
            h_r, h_i = carry
            t = jnp.where(dr == 0, s, tl - 1 - s)
            row = pl.multiple_of(t * B, B)
            n_r = lr * h_r - li * h_i + sre[pl.ds(row, B), cols]
            n_i = lr * h_i + li * h_r + sim[pl.ds(row, B), cols]
            sre[pl.ds(row, B), cols] = n_r
            sim[pl.ds(row, B), cols] = n_i
            return n_r, n_i

        h_r, h_i = lax.fori_loop(0, tl, body, (hre[:, cols], him[:, cols]), unroll=2)
        hre[:, cols] = h_r
        him[:, cols] = h_i

    hfin_ref[0, 0] = hre[...]
    hfin_ref[0, 1] = him[...]

    if emit_y:
        kout = cm_ref.shape[4]
        dsel = d_ref[...] * jnp.where(dr == 0, 1.0, 0.0)
        for n in range(nsb):
            y = (jnp.dot(sre[:, n * kst:(n + 1) * kst].astype(BF16), cm_ref[0, 0, n], preferred_element_type=F32)
                 + jnp.dot(sim[:, n * kst:(n + 1) * kst].astype(BF16), cm_ref[0, 1, n], preferred_element_type=F32))
            oc = slice(n * kout, (n + 1) * kout)
            y = y + dsel[:, oc] * u[:, oc]
            y_ref[0, :, :, oc] = y.reshape(tl, B, kout)


def _s5_scan(u3, h0, lam, wb, cm, d, *, emit_y):
    L, B, W = u3.shape
    SW = lam.shape[2]
    tl = min(32, L)
    nT = L // tl
    tmap = lambda dr, i: i + dr * (nT - 1 - 2 * i)
    in_specs = [pl.BlockSpec((tl, B, W), lambda dr, i: (tmap(dr, i), 0, 0)),
                pl.BlockSpec((1, 2, B, SW), lambda dr, i: (dr, 0, 0, 0)),
                pl.BlockSpec((1, 2, SW), lambda dr, i: (dr, 0, 0)),
                pl.BlockSpec((1,) + wb.shape[1:], lambda dr, i: (dr, 0, 0, 0, 0)),
                pl.BlockSpec((1,) + cm.shape[1:], lambda dr, i: (dr, 0, 0, 0, 0)),
                pl.BlockSpec((1, W), lambda dr, i: (0, 0))]
    out_specs = [pl.BlockSpec((1, 2, B, SW), lambda dr, i: (dr, 0, 0, 0))]
    out_shape = [jax.ShapeDtypeStruct((2, 2, B, SW), F32)]
    if emit_y:
        out_specs = [pl.BlockSpec((1, tl, B, W), lambda dr, i: (dr, tmap(dr, i), 0, 0))] + out_specs
        out_shape = [jax.ShapeDtypeStruct((2, L, B, W), F32)] + out_shape
    res = pl.pallas_call(
        functools.partial(_s5_kernel, tl=tl, nb=nT, cw=256, emit_y=emit_y),
        grid=(2, nT),
        in_specs=in_specs, out_specs=out_specs, out_shape=out_shape,
        scratch_shapes=[pltpu.VMEM((tl * B, SW), F32), pltpu.VMEM((tl * B, SW), F32),
                        pltpu.VMEM((B, SW), F32), pltpu.VMEM((B, SW), F32)],
        compiler_params=_cparams(("arbitrary", "arbitrary")),
        name="s5_scan_x" if emit_y else "s5_scan_ctx",
    )(u3, h0, lam, wb, cm, d.reshape(1, W))
    return res if emit_y else (None, res[0])


def _hy_filter_kernel(w1t_ref, w1c_ref, w1s_ref, b1_ref, wh_ref, bh_ref, fr_ref, wout_ref, dec_ref,
                      bands_ref, hs_ref, hd_ref, *, L, hy_w):
    pos = lax.broadcasted_iota(jnp.int32, (L, 1), 0).astype(F32)
    t = pos / max(L - 1, 1)
    ang = (2.0 * math.pi / L) * pos * bands_ref[...]
    fr = fr_ref[...]
    dot = functools.partial(jnp.dot, preferred_element_type=F32, precision=HIGHEST)
    pre = t * w1t_ref[...] + dot(jnp.cos(ang), w1c_ref[...]) + dot(-jnp.sin(ang), w1s_ref[...]) + b1_ref[...]
    hid = jnp.sin(fr * pre)
    for i in range(wh_ref.shape[0]):
        hid = jnp.sin(fr * (dot(hid, wh_ref[i]) + bh_ref[i]))
    h = dot(hid, wout_ref[...]) * jnp.exp(-t * jnp.abs(dec_ref[...]))
    nrm = jnp.sum(h * h, axis=0, keepdims=True)
    nrm = nrm[:, :hy_w] + nrm[:, hy_w:]
    scale = lax.rsqrt(nrm + EPS)
    hf = h[:, :hy_w] * scale
    hb = h[:, hy_w:] * scale
    hb0 = jnp.where(lax.broadcasted_iota(jnp.int32, hb.shape, 0) == 0, 0.0, hb)
    hs_ref[...] = hf + hb0
    hd_ref[...] = hf - hb0


def _hy_filters(L, w1, b1, wh, bh, freq, wout, decay, hy_w, order):
    nb = (w1.shape[0] - 1) // 2
    hid = w1.shape[1]
    bands = jnp.linspace(1e-4, nb - 1, nb, dtype=F32).reshape(1, nb)
    full = lambda shp: pl.BlockSpec(shp, lambda o: (0,) * len(shp))
    return pl.pallas_call(
        functools.partial(_hy_filter_kernel, L=L, hy_w=hy_w),
        grid=(order,),
        in_specs=[full((1, hid)), full((nb, hid)), full((nb, hid)), full((1, hid)),
                  full(wh.shape), full((wh.shape[0], 1, hid)), full((1, hid)),
                  pl.BlockSpec((hid, 2 * hy_w), lambda o: (0, o)),
                  pl.BlockSpec((1, 2 * hy_w), lambda o: (0, o)),
                  full((1, nb))],
        out_specs=[pl.BlockSpec((L, hy_w), lambda o: (0, o))] * 2,
        out_shape=[jax.ShapeDtypeStruct((L, order * hy_w), F32)] * 2,
        compiler_params=_cparams(("arbitrary",)),
        name="hyena_filters",
    )(w1[0:1], w1[1:1 + nb], w1[1 + nb:], b1.reshape(1, hid), wh, bh.reshape(wh.shape[0], 1, hid),
      freq.reshape(1, hid), wout, decay.reshape(1, -1), bands)


def _dft_kernel(c32_ref, s32_ref, cb_ref, s1_ref, s2_ref, *, L, rb):
    i = pl.program_id(0)
    f = lax.broadcasted_iota(jnp.int32, (rb, L), 0) + i * rb
    t = lax.broadcasted_iota(jnp.int32, (rb, L), 1)
    k = (f * t) % (2 * L)
    ang = k.astype(F32) * (math.pi / L)
    c = jnp.cos(ang)
    s = -jnp.sin(ang)
    c32_ref[...] = c
    s32_ref[...] = s
    cb_ref[...] = c.astype(BF16)
    sign_t = jnp.where(t % 2 == 0, 1.0, -1.0)
    sign_f = jnp.where(f % 2 == 0, 1.0, -1.0)
    s1_ref[...] = jnp.where(f == 0, sign_t, s).astype(BF16)
    s2_ref[...] = jnp.where(t == 0, sign_f, s).astype(BF16)


def _dft_mats(L):
    rb = min(256, L)
    spec = pl.BlockSpec((rb, L), lambda i: (i, 0))
    return pl.pallas_call(
        functools.partial(_dft_kernel, L=L, rb=rb),
        grid=(L // rb,),
        in_specs=[],
        out_specs=[spec] * 5,
        out_shape=[jax.ShapeDtypeStruct((L, L), F32)] * 2 + [jax.ShapeDtypeStruct((L, L), BF16)] * 3,
        compiler_params=_cparams(("arbitrary",)),
        name="dft_mats",
    )()


def _spec_kernel(c_ref, s_ref, hs_ref, hd_ref, kr_ref, ki_ref, kn_ref, *, L, rb):
    i = pl.program_id(1)
    f = lax.broadcasted_iota(jnp.int32, (rb, 1), 0) + i * rb
    w = jnp.where(f == 0, 1.0, 2.0) * (1.0 / (2 * L))
    dot = functools.partial(jnp.dot, preferred_element_type=F32, precision=HIGHEST)
    hs = hs_ref[...]
    kr_ref[0] = w * dot(c_ref[...], hs)
    ki_ref[0] = w * dot(s_ref[...], hd_ref[...])
    sign = jnp.where(lax.broadcasted_iota(jnp.int32, (L, 1), 0) % 2 == 0, 1.0, -1.0)
    kn_ref[0] = jnp.sum(sign * hs, axis=0, keepdims=True) * (1.0 / (2 * L))


def _hy_spectra(c32, s32, hs, hd, hy_w, order):
    L = c32.shape[0]
    rb = min(256, L)
    mat = pl.BlockSpec((rb, L), lambda o, i: (i, 0))
    hsp = pl.BlockSpec((L, hy_w), lambda o, i: (0, o))
    return pl.pallas_call(
        functools.partial(_spec_kernel, L=L, rb=rb),
        grid=(order, L // rb),
        in_specs=[mat, mat, hsp, hsp],
        out_specs=[pl.BlockSpec((1, rb, hy_w), lambda o, i: (o, i, 0))] * 2
                  + [pl.BlockSpec((1, 1, hy_w), lambda o, i: (o, 0, 0))],
        out_shape=[jax.ShapeDtypeStruct((order, L, hy_w), F32)] * 2
                  + [jax.ShapeDtypeStruct((order, 1, hy_w), F32)],
        compiler_params=_cparams(("arbitrary", "arbitrary")),
        name="hyena_spectra",
    )(c32, s32, hs, hd)


def _hy_conv_kernel(v_ref, x1_ref, x2_ref, c_ref, s_ref, kr_ref, ki_ref, kn_ref, d_ref, o_ref,
                    zf, zb, pre, pim, *, rb, order):
    o = pl.program_id(1)
    ph = pl.program_id(2)
    blk = pl.program_id(3)
    rows = pl.ds(pl.multiple_of(blk * rb, rb), rb)

    @pl.when((o == 0) & (ph == 0) & (blk == 0))
    def _():
        v = v_ref[...]
        zf[...] = v
        zb[...] = v.astype(BF16)

    @pl.when(ph == 0)
    def _():
        z = zb[...]
        zr = jnp.dot(c_ref[rows, :], z, preferred_element_type=F32)
        zi = jnp.dot(s_ref[0, rows, :], z, preferred_element_type=F32)
        kr, ki = kr_ref[0], ki_ref[0]
        first = (lax.broadcasted_iota(jnp.int32, zr.shape, 0) == 0) & (blk == 0)
        pr = kr * zr - jnp.where(first, 0.0, ki * zi)
        pi = jnp.where(first, kn_ref[0] * zi, kr * zi + ki * zr)
        pre[rows, :] = pr.astype(BF16)
        pim[rows, :] = pi.astype(BF16)

    @pl.when(ph == 1)
    def _():
        y = (jnp.dot(c_ref[rows, :], pre[...], preferred_element_type=F32)
             + jnp.dot(s_ref[1, rows, :], pim[...], preferred_element_type=F32))
        zin = zf[rows, :]
        gate = jnp.where(o == 0, x1_ref[rows, :], x2_ref[rows, :])
        zn = gate * (y + d_ref[0] * zin)
        zf[rows, :] = zn
        zb[rows, :] = zn.astype(BF16)

        @pl.when(o == order - 1)
        def _():
            o_ref[rows, :] = zn


def _hy_conv(v, x1, x2, cb, s12, kr, ki, kn, d_hy, hy_w):
    L, BC = v.shape
    order = kr.shape[0]
    nc = 256
    rb = min(1024, L)
    nblk = L // rb
    nct = hy_w // nc
    col = pl.BlockSpec((L, nc), lambda c, o, p, k: (0, c))
    kmap = lambda c, o, p, k: (o, k * (1 - p) + (nblk - 1) * p, c % nct)
    return pl.pallas_call(
        functools.partial(_hy_conv_kernel, rb=rb, order=order),
        grid=(BC // nc, order, 2, nblk),
        in_specs=[col, col, col,
                  pl.BlockSpec((L, L), lambda c, o, p, k: (0, 0), pipeline_mode=pl.Buffered(1)),
                  pl.BlockSpec((2, L, L), lambda c, o, p, k: (0, 0, 0), pipeline_mode=pl.Buffered(1)),
                  pl.BlockSpec((1, rb, nc), kmap),
                  pl.BlockSpec((1, rb, nc), kmap),
                  pl.BlockSpec((1, 1, nc), lambda c, o, p, k: (o, 0, c % nct)),
                  pl.BlockSpec((1, 1, nc), lambda c, o, p, k: (o, 0, c % nct))],
        out_specs=col,
        out_shape=jax.ShapeDtypeStruct((L, BC), F32),
        scratch_shapes=[pltpu.VMEM((L, nc), F32), pltpu.VMEM((L, nc), BF16),
                        pltpu.VMEM((L, nc), BF16), pltpu.VMEM((L, nc), BF16)],
        compiler_params=_cparams(("arbitrary",) * 4),
        name="hyena_conv",
    )(v, x1, x2, cb, s12, kr, ki, kn, d_hy.reshape(order, 1, hy_w))


def _merge_kernel(y_ref, z_ref, x_ref, mod_ref, wg_ref, bg_ref, g5_ref, gh_ref, wo5_ref, woh_ref, g2_ref,
                  x1_ref, h2t_ref):
    y = y_ref[0] + y_ref[1]
    gy = _gelu(y)
    gate = jnp.dot(gy.astype(BF16), wg_ref[...], preferred_element_type=F32) + bg_ref[...]
    s5 = gy * _sigmoid(gate)
    n5 = _rms(s5, g5_ref[...])
    nh = _rms(z_ref[...], gh_ref[...])
    mix = (jnp.dot(n5.astype(BF16), wo5_ref[...], preferred_element_type=F32)
           + jnp.dot(nh.astype(BF16), woh_ref[...], preferred_element_type=F32))
    x1 = x_ref[0] + mod_ref[0, 2:3, :] * mix
    x1_ref[0] = x1
    h2 = _rms(x1, g2_ref[...]) * (1.0 + mod_ref[0, 4:5, :]) + mod_ref[0, 3:4, :]
    h2t_ref[0] = h2.T.astype(BF16)


def _merge(y, z, x, mod3, w_glu, b_glu, g5, gh, w_out, g2):
    B, L, D = x.shape
    W5 = w_glu.shape[0]
    WH = z.shape[1] // B
    tl = min(512, L)
    full = lambda shp: pl.BlockSpec(shp, lambda b_, t: (0,) * len(shp))
    return pl.pallas_call(
        _merge_kernel,
        grid=(B, L // tl),
        in_specs=[pl.BlockSpec((2, tl, W5), lambda b_, t: (0, t, b_)),
                  pl.BlockSpec((tl, WH), lambda b_, t: (t, b_)),
                  pl.BlockSpec((1, tl, D), lambda b_, t: (b_, t, 0)),
                  pl.BlockSpec((1, N_MOD, D), lambda b_, t: (b_, 0, 0)),
                  full((W5, W5)), full((1, W5)), full((1, W5)), full((1, WH)),
                  full((W5, D)), full((WH, D)), full((1, D))],
        out_specs=[pl.BlockSpec((1, tl, D), lambda b_, t: (b_, t, 0)),
                   pl.BlockSpec((1, D, tl), lambda b_, t: (b_, 0, t))],
        out_shape=[jax.ShapeDtypeStruct((B, L, D), F32), jax.ShapeDtypeStruct((B, D, L), BF16)],
        compiler_params=_cparams(("arbitrary", "arbitrary")),
        name="merge_heads",
    )(y, z, x, mod3, w_glu.astype(BF16), b_glu.reshape(1, W5), g5.reshape(1, W5), gh.reshape(1, WH),
      w_out[:W5].astype(BF16), w_out[W5:].astype(BF16), g2.reshape(1, D))


def _topk_rank(s, k):
    vals = []
    cur = s
    rank = jnp.full(s.shape, float(k), F32)
    for i in range(k):
        m = jnp.max(cur, axis=0, keepdims=True)
        hit = cur == m
        vals.append(m)
        cur = jnp.where(hit, -jnp.inf, cur)
        rank = jnp.where(hit, float(i), rank)
    return jnp.concatenate(vals, axis=0), rank


def _topk_desc(s, k):
    vals = []
    cur = s
    for _ in range(k):
        m = jnp.max(cur, axis=0, keepdims=True)
        vals.append(m)
        cur = jnp.where(cur == m, -jnp.inf, cur)
    return jnp.concatenate(vals, axis=0)


def _gelu_packed(x):
    c = math.sqrt(2.0 / math.pi)
    inner = x * (c + (c * 0.044715) * (x * x))
    hx = 0.5 * x
    return hx + hx * jnp.tanh(inner)


PACK = 16
LANES = 128
SUBLANES = 8


def _packed_row(ref, lead, row, n_tiles):
    r8 = jnp.concatenate([ref[lead + (c, pl.ds(row, SUBLANES, stride=0), slice(None))]
                          for c in range(n_tiles)], axis=1)
    return jnp.concatenate([r8, r8], axis=0).astype(BF16)


def _peer_kernel(h_ref, x1_ref, mod_ref, wq_ref, k1_ref, k2_ref, u_ref, vt_ref, gf_ref, o_ref,
                 ea, cs, eb, r2b, acc, a0, a1, g0, g1, *, nb1, ne, topk):
    f = pl.program_id(0)
    total = pl.num_programs(0) - 2
    H, NK, _ = k1_ref.shape
    T = h_ref.shape[2]
    f2 = jnp.clip(f - 1, 0, total - 1)
    f3 = jnp.clip(f - 2, 0, total - 1)
    par2 = (f2 // ne) % 2
    par3 = (f3 // ne) % 2
    e_mid = f2 % ne

    @pl.when(f == 0)
    def _():
        for buf in (a0, a1, g0, g1):
            buf[...] = jnp.zeros_like(buf)

    @pl.when((f % ne == 0) & (f < total))
    def _():
        par = (f // ne) % 2
        acc[par] = jnp.zeros(acc.shape[1:], F32)
        qt = jnp.dot(wq_ref[...], h_ref[0], preferred_element_type=F32)
        dk = k1_ref.shape[2]
        for hh in range(H):
            q1 = qt[(2 * hh) * dk:(2 * hh + 1) * dk, :]
            q2 = qt[(2 * hh + 1) * dk:(2 * hh + 2) * dk, :]
            s1f = jnp.dot(k1_ref[hh], q1, preferred_element_type=F32, precision=HIGHEST)
            s2f = jnp.dot(k2_ref[hh], q2, preferred_element_type=F32, precision=HIGHEST)
            for c0 in range(0, T, LANES):
                lc = slice(c0, c0 + LANES)
                s1, s2 = s1f[:, lc], s2f[:, lc]
                v1 = _topk_desc(s1, topk)
                v2, r2 = _topk_rank(s2, topk)
                cands = [v1[a:a + 1] + v2[b:b + 1]
                         for a in range(topk) for b in range(topk) if (a + 1) * (b + 1) <= topk]
                sel = _topk_desc(jnp.concatenate(cands, axis=0), topk)
                tau = sel[topk - 1:topk]
                mx = sel[0:1]
                rz = 1.0 / jnp.sum(jnp.exp(sel - mx), axis=0, keepdims=True)
                cnt = jnp.zeros((NK, LANES), F32)
                for a_ in range(topk):
                    c_a = jnp.sum(jnp.where(v1[a_:a_ + 1] + v2 >= tau, 1.0, 0.0), axis=0, keepdims=True)
                    cnt = jnp.where(s1 == v1[a_:a_ + 1], c_a, cnt)
                ea[par, hh, c0 // LANES] = jnp.exp(s1 - v1[0:1]) * rz
                cs[par, hh, c0 // LANES] = cnt
                eb[par, hh, :, lc] = jnp.exp(s2 - v2[0:1]).astype(BF16)
                r2b[par, hh, :, lc] = r2.astype(BF16)

    def stages(a_new, a_old, g_new, g_old):
        th = T // 2
        neb = nb1 * NK
        pieces = []
        for half in range(2):
            tc = slice(half * th, (half + 1) * th)
            for m0 in range(half * neb // 2, (half + 1) * neb // 2, MXU_ROWS):
                pieces.append(("a", slice(m0, m0 + MXU_ROWS), slice(None)))
            for m0 in range(0, vt_ref.shape[0], MXU_ROWS):
                pieces.append(("acc", slice(m0, m0 + MXU_ROWS), tc))

        def mxu_piece(kind, ms, tc):
            if kind == "a":
                a_new[ms, tc] = jnp.dot(u_ref[ms, :], h_ref[0, :, tc], preferred_element_type=F32)
            else:
                acc[par3, ms, tc] += jnp.dot(vt_ref[ms, :], g_old[:, tc], preferred_element_type=F32)

        def vpu_unit(j, r_lo, r_hi):
            i1 = e_mid * nb1 + j
            rss = [slice(r * PACK, (r + 1) * PACK) for r in range(r_lo, r_hi)]
            ws = [jnp.zeros((PACK, T), BF16) for _ in rss]
            for hh in range(H):
                ea_p = _packed_row(ea, (par2, hh), i1, T // LANES)
                cs_p = _packed_row(cs, (par2, hh), i1, T // LANES)
                for k, rs in enumerate(rss):
                    ws[k] = ws[k] + jnp.where(r2b[par2, hh, rs, :] < cs_p, ea_p * eb[par2, hh, rs, :],
                                              jnp.zeros((), BF16))
            for k, rs in enumerate(rss):
                rows = slice(j * NK + rs.start, j * NK + rs.stop)
                g_new[rows, :] = _gelu_packed(a_old[rows, :].astype(BF16)) * ws[k]

        nr = NK // PACK
        units = [(j, r0, r0 + VPU_UNIT) for j in range(nb1) for r0 in range(0, nr, VPU_UNIT)]
        weight = [u_ref.shape[1] if kind == "a" else neb for kind, _, _ in pieces]
        done = 0
        for k, piece in enumerate(pieces):
            mxu_piece(*piece)
            upto = len(units) * sum(weight[:k + 1]) // sum(weight)
            for unit in units[done:upto]:
                vpu_unit(*unit)
            done = upto

    @pl.when(f % 2 == 0)
    def _():
        stages(a0, a1, g1, g0)

    @pl.when(f % 2 == 1)
    def _():
        stages(a1, a0, g0, g1)

    @pl.when((f >= 2) & ((f - 1) % ne == 0))
    def _():
        x2 = x1_ref[0] + mod_ref[0, 5:6, :] * acc[par3].T
        o_ref[0] = _rms(x2, gf_ref[...])


def _peer(h2t, x1, mod3, wq, k1, k2, u_tab, v_tab, g_final):
    B, L, D = x1.shape
    H, NK, dk = k1.shape
    T = min(512, L)
    nb1 = 8
    nlt = L // T
    ne = NK // nb1
    total = B * nlt * ne
    t_in = lambda f: jnp.minimum(f // ne, B * nlt - 1)
    t_out = lambda f: jnp.maximum(f - 2, 0) // ne
    once = lambda shp: pl.BlockSpec(shp, lambda f: (0,) * len(shp), pipeline_mode=pl.Buffered(1))
    tok = pl.BlockSpec((1, T, D), lambda f: (t_out(f) // nlt, t_out(f) % nlt, 0))
    return pl.pallas_call(
        functools.partial(_peer_kernel, nb1=nb1, ne=ne, topk=PEER_TOPK),
        grid=(total + 2,),
        in_specs=[pl.BlockSpec((1, D, T), lambda f: (t_in(f) // nlt, 0, t_in(f) % nlt)), tok,
                  pl.BlockSpec((1, N_MOD, D), lambda f: (t_out(f) // nlt, 0, 0)),
                  once(wq.shape[::-1]), once(k1.shape), once(k2.shape),
                  pl.BlockSpec((nb1 * NK, D), lambda f: (jnp.minimum(f, total - 1) % ne, 0)),
                  pl.BlockSpec((D, nb1 * NK), lambda f: (0, jnp.maximum(f - 2, 0) % ne)),
                  once((1, D))],
        out_specs=tok,
        out_shape=jax.ShapeDtypeStruct((B, L, D), F32),
        scratch_shapes=[pltpu.VMEM((2, H, T // LANES, NK, LANES), F32),
                        pltpu.VMEM((2, H, T // LANES, NK, LANES), F32),
                        pltpu.VMEM((2, H, NK, T), BF16), pltpu.VMEM((2, H, NK, T), BF16),
                        pltpu.VMEM((2, D, T), F32),
                        pltpu.VMEM((nb1 * NK, T), F32), pltpu.VMEM((nb1 * NK, T), F32),
                        pltpu.VMEM((nb1 * NK, T), BF16), pltpu.VMEM((nb1 * NK, T), BF16)],
        compiler_params=_cparams(("arbitrary",)),
        name="peer_dense",
    )(h2t, x1, mod3, wq.astype(BF16).T, k1, k2, u_tab.astype(BF16), v_tab.astype(BF16).T, g_final.reshape(1, D))


def kernel(x, c, ctx, c_ctx, w_ada, b_ada, g_norm1, g_norm2, w_in, b_in, s5_a_re, s5_a_im, s5_log_step, s5_b_re, s5_b_im, s5_c_re, s5_c_im, s5_d, w_glu, b_glu, hy_conv_w, hy_conv_b, hf_w1, hf_b1, hf_wh, hf_bh, hf_freq, hf_wout, hf_decay, hy_d, g_out_s5, g_out_hy, w_out, peer_wq, peer_k1, peer_k2, peer_u, peer_v, g_final):
    B, L, D = x.shape
    Lc = ctx.shape[1]
    depth = w_ada.shape[0]
    assert depth == 1, "single-layer block"
    l = 0
    S5W = w_glu.shape[1]
    HYW = g_out_hy.shape[1]
    order = hy_d.shape[1]
    _, _, G, P, Hs = s5_b_re.shape

    rpad = (-(B + 1)) % 8
    cc = jnp.concatenate([c, c_ctx[None], jnp.zeros((rpad, D), F32)], axis=0)
    mod = _ada(cc, w_ada[l], b_ada[l])
    mod_x = mod[:B].reshape(B, N_MOD, D)
    mod_c = mod[B:B + 1].reshape(1, N_MOD, D)

    w_in_bf = w_in[l].astype(BF16)
    u_x, v, xg1, xg2 = _inproj(x, mod_x, g_norm1[l], w_in_bf, b_in[l], hy_conv_w[l], hy_conv_b[l],
                               s5_w=S5W, hy_w=HYW, hyena=True)
    (u_c,) = _inproj(ctx, mod_c, g_norm1[l], w_in_bf[:, :S5W], b_in[l][:S5W], None, None,
                     s5_w=S5W, hy_w=HYW, hyena=False)

    lr, li, bbr, bbi = _s5_prep(s5_a_re[l], s5_a_im[l], s5_log_step[l], s5_b_re[l], s5_b_im[l])
    gin = MXU_W // Hs
    lam = jnp.stack([lr.reshape(2, G * P), li.reshape(2, G * P)], axis=1)
    tr = lambda m: jnp.swapaxes(m, 2, 3)
    wb = jnp.stack([_block_diag(tr(bbr), gin), _block_diag(tr(bbi), gin)], axis=1).astype(BF16)
    cm = jnp.stack([_block_diag(tr(s5_c_re[l]), gin), _block_diag(-tr(s5_c_im[l]), gin)], axis=1).astype(BF16)
    h_zero = jnp.zeros((2, 2, B, G * P), F32)
    _, h_ctx = _s5_scan(u_c.reshape(Lc, B, S5W), h_zero, lam, wb, cm, s5_d[l], emit_y=False)
    y5, _ = _s5_scan(u_x.reshape(L, B, S5W), h_ctx, lam, wb, cm, s5_d[l], emit_y=True)

    hs, hd = _hy_filters(L, hf_w1[l], hf_b1[l], hf_wh[l], hf_bh[l], hf_freq[l], hf_wout[l], hf_decay[l],
                         HYW, order)
    c32, s32, cb, s1b, s2b = _dft_mats(L)
    kr, ki, kn = _hy_spectra(c32, s32, hs, hd, HYW, order)
    z_hy = _hy_conv(v, xg1, xg2, cb, jnp.stack([s1b, s2b]), kr, ki, kn, hy_d[l], HYW)

    x1, h2 = _merge(y5.reshape(2, L, B * S5W), z_hy, x, mod_x, w_glu[l], b_glu[l],
                    g_out_s5[l], g_out_hy[l], w_out[l], g_norm2[l])
    return _peer(h2, x1, mod_x, peer_wq[l], peer_k1[l], peer_k2[l], peer_u[l], peer_v[l], g_final)
```

```python
import functools
import math

import jax
import jax.numpy as jnp
from jax import lax
from jax.experimental import pallas as pl
from jax.experimental.pallas import tpu as pltpu

F32 = jnp.float32
BF16 = jnp.bfloat16
HIGHEST = lax.Precision.HIGHEST

EPS = 1e-6
GRID_W = 64
PEER_TOPK = 16
N_MOD = 6

MXU_W = 256
MXU_ROWS = 512
VPU_UNIT = 8
VMEM_LIMIT = 56 * 1024 * 1024


def _cparams(sem):
    return pltpu.CompilerParams(dimension_semantics=sem, vmem_limit_bytes=VMEM_LIMIT)


def _gelu(x):
    c = math.sqrt(2.0 / math.pi)
    return 0.5 * x * (1.0 + jnp.tanh(c * (x + 0.044715 * (x * x * x))))


def _sigmoid(x):
    return 1.0 / (1.0 + jnp.exp(-x))


def _rms(x, g):
    return x * lax.rsqrt(jnp.mean(x * x, axis=-1, keepdims=True) + EPS) * g


def _ada_kernel(c_ref, w_ref, b_ref, o_ref):
    c = c_ref[...]
    s = c * _sigmoid(c)
    o_ref[...] = jnp.dot(s, w_ref[...], preferred_element_type=F32, precision=HIGHEST) + b_ref[...]


def _ada(cc, w, b):
    R, D = cc.shape
    N = w.shape[1]
    tn = 1024
    return pl.pallas_call(
        _ada_kernel,
        grid=(N // tn,),
        in_specs=[pl.BlockSpec((R, D), lambda j: (0, 0)),
                  pl.BlockSpec((D, tn), lambda j: (0, j)),
                  pl.BlockSpec((1, tn), lambda j: (0, j))],
        out_specs=pl.BlockSpec((R, tn), lambda j: (0, j)),
        out_shape=jax.ShapeDtypeStruct((R, N), F32),
        compiler_params=_cparams(("arbitrary",)),
        name="ada_mod",
    )(cc, w, b.reshape(1, N))


def _inproj_kernel(x_ref, mod_ref, g_ref, w_ref, b_ref, *rest, s5_w, hy_w, hyena):
    x = x_ref[0]
    h = _rms(x, g_ref[...])
    h = h * (1.0 + mod_ref[0, 1:2, :]) + mod_ref[0, 0:1, :]
    p = jnp.dot(h.astype(BF16), w_ref[...], preferred_element_type=F32) + b_ref[...]
    if not hyena:
        (u_ref,) = rest
        u_ref[...] = p
        return
    cw_ref, cb_ref, u_ref, v_ref, x1_ref, x2_ref = rest
    u_ref[...] = p[:, :s5_w]
    ph = p[:, s5_w:]
    tl = ph.shape[0]
    pos = lax.broadcasted_iota(jnp.int32, ph.shape, 0) % GRID_W
    prev = jnp.where(pos == 0, 0.0, pltpu.roll(ph, 1, axis=0))
    nxt = jnp.where(pos == GRID_W - 1, 0.0, pltpu.roll(ph, tl - 1, axis=0))
    q = cb_ref[...] + cw_ref[0:1, :] * prev + cw_ref[1:2, :] * ph + cw_ref[2:3, :] * nxt
    v_ref[...] = q[:, :hy_w]
    x1_ref[...] = q[:, hy_w:2 * hy_w]
    x2_ref[...] = q[:, 2 * hy_w:]


def _inproj(x, mod3, g, w_bf, b, conv_w, conv_b, *, s5_w, hy_w, hyena):
    B, L, D = x.shape
    N = w_bf.shape[1]
    tl = min(512, L)
    nb = mod3.shape[0]
    mod_map = (lambda b_, t: (b_, 0, 0)) if nb > 1 else (lambda b_, t: (0, 0, 0))
    in_specs = [pl.BlockSpec((1, tl, D), lambda b_, t: (b_, t, 0)),
                pl.BlockSpec((1, N_MOD, D), mod_map),
                pl.BlockSpec((1, D), lambda b_, t: (0, 0)),
                pl.BlockSpec((D, N), lambda b_, t: (0, 0)),
                pl.BlockSpec((1, N), lambda b_, t: (0, 0))]
    args = [x, mod3, g.reshape(1, D), w_bf, b.reshape(1, N)]
    out_specs = [pl.BlockSpec((tl, s5_w), lambda b_, t: (t, b_))]
    out_shape = [jax.ShapeDtypeStruct((L, B * s5_w), F32)]
    if hyena:
        nh = N - s5_w
        in_specs += [pl.BlockSpec((3, nh), lambda b_, t: (0, 0)),
                     pl.BlockSpec((1, nh), lambda b_, t: (0, 0))]
        args += [conv_w, conv_b.reshape(1, nh)]
        out_specs += [pl.BlockSpec((tl, hy_w), lambda b_, t: (t, b_))] * 3
        out_shape += [jax.ShapeDtypeStruct((L, B * hy_w), F32)] * 3
    return pl.pallas_call(
        functools.partial(_inproj_kernel, s5_w=s5_w, hy_w=hy_w, hyena=hyena),
        grid=(B, L // tl),
        in_specs=in_specs, out_specs=out_specs, out_shape=out_shape,
        compiler_params=_cparams(("arbitrary", "arbitrary")),
        name="inproj_hy" if hyena else "inproj_ctx",
    )(*args)


def _s5_prep_kernel(are_ref, aim_ref, ls_ref, bre_ref, bim_ref, lr_ref, li_ref, bbr_ref, bbi_ref):
    a_re, a_im = are_ref[...], aim_ref[...]
    step = jnp.exp(ls_ref[...])
    er = jnp.exp(a_re * step)
    lr = er * jnp.cos(a_im * step)
    li = er * jnp.sin(a_im * step)
    nr, ni = lr - 1.0, li
    d2 = a_re * a_re + a_im * a_im
    kr = (nr * a_re + ni * a_im) / d2
    ki = (ni * a_re - nr * a_im) / d2
    br, bi = bre_ref[...], bim_ref[...]
    lr_ref[...] = lr
    li_ref[...] = li
    bbr_ref[...] = kr * br - ki * bi
    bbi_ref[...] = kr * bi + ki * br


def _s5_prep(a_re, a_im, log_step, b_re, b_im):
    two, G, P, H = b_re.shape
    n = two * G * P
    col = lambda a: a.reshape(n, 1)
    ls = jnp.broadcast_to(log_step[:, :, None], (two, G, P))
    outs = pl.pallas_call(
        _s5_prep_kernel,
        out_shape=[jax.ShapeDtypeStruct((n, 1), F32)] * 2 + [jax.ShapeDtypeStruct((n, H), F32)] * 2,
        name="s5_prep",
    )(col(a_re), col(a_im), col(ls), b_re.reshape(n, H), b_im.reshape(n, H))
    lr, li, bbr, bbi = outs
    return (lr.reshape(two, G, P), li.reshape(two, G, P),
            bbr.reshape(two, G, P, H), bbi.reshape(two, G, P, H))


def _block_diag(m, gb):
    two, G, R, C = m.shape
    m = m.reshape(two, G // gb, gb, R, C)
    eye = jnp.eye(gb, dtype=m.dtype)
    out = m[:, :, :, :, None, :] * eye[None, None, :, None, :, None]
    return out.reshape(two, G // gb, gb * R, gb * C)


def _s5_kernel(u_ref, h0_ref, lam_ref, wb_ref, cm_ref, d_ref, *rest, tl, nb, cw, emit_y):
    if emit_y:
        y_ref, hfin_ref, sre, sim, hre, him = rest
    else:
        hfin_ref, sre, sim, hre, him = rest
    dr = pl.program_id(0)
    i = pl.program_id(1)
    B = u_ref.shape[1]
    SW = sre.shape[1]
    nsb = wb_ref.shape[2]
    kin = wb_ref.shape[3]
    kst = wb_ref.shape[4]

    @pl.when(i == 0)
    def _():
        hre[...] = h0_ref[0, 0]
        him[...] = h0_ref[0, 1]

    u = u_ref[...].reshape(tl * B, u_ref.shape[2])
    ub = u.astype(BF16)
    for j in range(nsb):
        uj = ub[:, j * kin:(j + 1) * kin]
        sre[:, j * kst:(j + 1) * kst] = jnp.dot(uj, wb_ref[0, 0, j], preferred_element_type=F32)
        sim[:, j * kst:(j + 1) * kst] = jnp.dot(uj, wb_ref[0, 1, j], preferred_element_type=F32)

    for c in range(SW // cw):
        cols = slice(c * cw, (c + 1) * cw)
        lr = jnp.broadcast_to(lam_ref[0, 0:1, cols], (B, cw))
        li = jnp.broadcast_to(lam_ref[0, 1:2, cols], (B, cw))

        def body(s, carry, cols=cols, lr=lr, li=li):
            h_r, h_i = carry
            t = jnp.where(dr == 0, s, tl - 1 - s)
            row = pl.multiple_of(t * B, B)
            n_r = lr * h_r - li * h_i + sre[pl.ds(row, B), cols]
            n_i = lr * h_i + li * h_r + sim[pl.ds(row, B), cols]
            sre[pl.ds(row, B), cols] = n_r
            sim[pl.ds(row, B), cols] = n_i
            return n_r, n_i

        h_r, h_i = lax.fori_loop(0, tl, body, (hre[:, cols], him[:, cols]), unroll=2)
        hre[:, cols] = h_r
        him[:, cols] = h_i

    hfin_ref[0, 0] = hre[...]
    hfin_ref[0, 1] = him[...]

    if emit_y:
        kout = cm_ref.shape[4]
        dsel = d_ref[...] * jnp.where(dr == 0, 1.0, 0.0)
        for n in range(nsb):
            y = (jnp.dot(sre[:, n * kst:(n + 1) * kst].astype(BF16), cm_ref[0, 0, n], preferred_element_type=F32)
                 + jnp.dot(sim[:, n * kst:(n + 1) * kst].astype(BF16), cm_ref[0, 1, n], preferred_element_type=F32))
            oc = slice(n * kout, (n + 1) * kout)
            y = y + dsel[:, oc] * u[:, oc]
            y_ref[0, :, :, oc] = y.reshape(tl, B, kout)


def _s5_scan(u3, h0, lam, wb, cm, d, *, emit_y):
    L, B, W = u3.shape
    SW = lam.shape[2]
    tl = min(32, L)
    nT = L // tl
    tmap = lambda dr, i: i + dr * (nT - 1 - 2 * i)
    in_specs = [pl.BlockSpec((tl, B, W), lambda dr, i: (tmap(dr, i), 0, 0)),
                pl.BlockSpec((1, 2, B, SW), lambda dr, i: (dr, 0, 0, 0)),
                pl.BlockSpec((1, 2, SW), lambda dr, i: (dr, 0, 0)),
                pl.BlockSpec((1,) + wb.shape[1:], lambda dr, i: (dr, 0, 0, 0, 0)),
                pl.BlockSpec((1,) + cm.shape[1:], lambda dr, i: (dr, 0, 0, 0, 0)),
                pl.BlockSpec((1, W), lambda dr, i: (0, 0))]
    out_specs = [pl.BlockSpec((1, 2, B, SW), lambda dr, i: (dr, 0, 0, 0))]
    out_shape = [jax.ShapeDtypeStruct((2, 2, B, SW), F32)]
    if emit_y:
        out_specs = [pl.BlockSpec((1, tl, B, W), lambda dr, i: (dr, tmap(dr, i), 0, 0))] + out_specs
        out_shape = [jax.ShapeDtypeStruct((2, L, B, W), F32)] + out_shape
    res = pl.pallas_call(
        functools.partial(_s5_kernel, tl=tl, nb=nT, cw=256, emit_y=emit_y),
        grid=(2, nT),
        in_specs=in_specs, out_specs=out_specs, out_shape=out_shape,
        scratch_shapes=[pltpu.VMEM((tl * B, SW), F32), pltpu.VMEM((tl * B, SW), F32),
                        pltpu.VMEM((B, SW), F32), pltpu.VMEM((B, SW), F32)],
        compiler_params=_cparams(("arbitrary", "arbitrary")),
        name="s5_scan_x" if emit_y else "s5_scan_ctx",
    )(u3, h0, lam, wb, cm, d.reshape(1, W))
    return res if emit_y else (None, res[0])


def _hy_filter_kernel(w1t_ref, w1c_ref, w1s_ref, b1_ref, wh_ref, bh_ref, fr_ref, wout_ref, dec_ref,
                      bands_ref, hs_ref, hd_ref, *, L, hy_w):
    pos = lax.broadcasted_iota(jnp.int32, (L, 1), 0).astype(F32)
    t = pos / max(L - 1, 1)
    ang = (2.0 * math.pi / L) * pos * bands_ref[...]
    fr = fr_ref[...]
    dot = functools.partial(jnp.dot, preferred_element_type=F32, precision=HIGHEST)
    pre = t * w1t_ref[...] + dot(jnp.cos(ang), w1c_ref[...]) + dot(-jnp.sin(ang), w1s_ref[...]) + b1_ref[...]
    hid = jnp.sin(fr * pre)
    for i in range(wh_ref.shape[0]):
        hid = jnp.sin(fr * (dot(hid, wh_ref[i]) + bh_ref[i]))
    h = dot(hid, wout_ref[...]) * jnp.exp(-t * jnp.abs(dec_ref[...]))
    nrm = jnp.sum(h * h, axis=0, keepdims=True)
    nrm = nrm[:, :hy_w] + nrm[:, hy_w:]
    scale = lax.rsqrt(nrm + EPS)
    hf = h[:, :hy_w] * scale
    hb = h[:, hy_w:] * scale
    hb0 = jnp.where(lax.broadcasted_iota(jnp.int32, hb.shape, 0) == 0, 0.0, hb)
    hs_ref[...] = hf + hb0
    hd_ref[...] = hf - hb0


def _hy_filters(L, w1, b1, wh, bh, freq, wout, decay, hy_w, order):
    nb = (w1.shape[0] - 1) // 2
    hid = w1.shape[1]
    bands = jnp.linspace(1e-4, nb - 1, nb, dtype=F32).reshape(1, nb)
    full = lambda shp: pl.BlockSpec(shp, lambda o: (0,) * len(shp))
    return pl.pallas_call(
        functools.partial(_hy_filter_kernel, L=L, hy_w=hy_w),
        grid=(order,),
        in_specs=[full((1, hid)), full((nb, hid)), full((nb, hid)), full((1, hid)),
                  full(wh.shape), full((wh.shape[0], 1, hid)), full((1, hid)),
                  pl.BlockSpec((hid, 2 * hy_w), lambda o: (0, o)),
                  pl.BlockSpec((1, 2 * hy_w), lambda o: (0, o)),
                  full((1, nb))],
        out_specs=[pl.BlockSpec((L, hy_w), lambda o: (0, o))] * 2,
        out_shape=[jax.ShapeDtypeStruct((L, order * hy_w), F32)] * 2,
        compiler_params=_cparams(("arbitrary",)),
        name="hyena_filters",
    )(w1[0:1], w1[1:1 + nb], w1[1 + nb:], b1.reshape(1, hid), wh, bh.reshape(wh.shape[0], 1, hid),
      freq.reshape(1, hid), wout, decay.reshape(1, -1), bands)


def _dft_kernel(c32_ref, s32_ref, cb_ref, s1_ref, s2_ref, *, L, rb):
    i = pl.program_id(0)
    f = lax.broadcasted_iota(jnp.int32, (rb, L), 0) + i * rb
    t = lax.broadcasted_iota(jnp.int32, (rb, L), 1)
    k = (f * t) % (2 * L)
    ang = k.astype(F32) * (math.pi / L)
    c = jnp.cos(ang)
    s = -jnp.sin(ang)
    c32_ref[...] = c
    s32_ref[...] = s
    cb_ref[...] = c.astype(BF16)
    sign_t = jnp.where(t % 2 == 0, 1.0, -1.0)
    sign_f = jnp.where(f % 2 == 0, 1.0, -1.0)
    s1_ref[...] = jnp.where(f == 0, sign_t, s).astype(BF16)
    s2_ref[...] = jnp.where(t == 0, sign_f, s).astype(BF16)


def _dft_mats(L):
    rb = min(256, L)
    spec = pl.BlockSpec((rb, L), lambda i: (i, 0))
    return pl.pallas_call(
        functools.partial(_dft_kernel, L=L, rb=rb),
        grid=(L // rb,),
        in_specs=[],
        out_specs=[spec] * 5,
        out_shape=[jax.ShapeDtypeStruct((L, L), F32)] * 2 + [jax.ShapeDtypeStruct((L, L), BF16)] * 3,
        compiler_params=_cparams(("arbitrary",)),
        name="dft_mats",
    )()


def _spec_kernel(c_ref, s_ref, hs_ref, hd_ref, kr_ref, ki_ref, kn_ref, *, L, rb):
    i = pl.program_id(1)
    f = lax.broadcasted_iota(jnp.int32, (rb, 1), 0) + i * rb
    w = jnp.where(f == 0, 1.0, 2.0) * (1.0 / (2 * L))
    dot = functools.partial(jnp.dot, preferred_element_type=F32, precision=HIGHEST)
    hs = hs_ref[...]
    kr_ref[0] = w * dot(c_ref[...], hs)
    ki_ref[0] = w * dot(s_ref[...], hd_ref[...])
    sign = jnp.where(lax.broadcasted_iota(jnp.int32, (L, 1), 0) % 2 == 0, 1.0, -1.0)
    kn_ref[0] = jnp.sum(sign * hs, axis=0, keepdims=True) * (1.0 / (2 * L))


def _hy_spectra(c32, s32, hs, hd, hy_w, order):
    L = c32.shape[0]
    rb = min(256, L)
    mat = pl.BlockSpec((rb, L), lambda o, i: (i, 0))
    hsp = pl.BlockSpec((L, hy_w), lambda o, i: (0, o))
    return pl.pallas_call(
        functools.partial(_spec_kernel, L=L, rb=rb),
        grid=(order, L // rb),
        in_specs=[mat, mat, hsp, hsp],
        out_specs=[pl.BlockSpec((1, rb, hy_w), lambda o, i: (o, i, 0))] * 2
                  + [pl.BlockSpec((1, 1, hy_w), lambda o, i: (o, 0, 0))],
        out_shape=[jax.ShapeDtypeStruct((order, L, hy_w), F32)] * 2
                  + [jax.ShapeDtypeStruct((order, 1, hy_w), F32)],
        compiler_params=_cparams(("arbitrary", "arbitrary")),
        name="hyena_spectra",
    )(c32, s32, hs, hd)


def _hy_conv_kernel(v_ref, x1_ref, x2_ref, c_ref, s_ref, kr_ref, ki_ref, kn_ref, d_ref, o_ref,
                    zf, zb, pre, pim, *, rb, order):
    o = pl.program_id(1)
    ph = pl.program_id(2)
    blk = pl.program_id(3)
    rows = pl.ds(pl.multiple_of(blk * rb, rb), rb)

    @pl.when((o == 0) & (ph == 0) & (blk == 0))
    def _():
        v = v_ref[...]
        zf[...] = v
        zb[...] = v.astype(BF16)

    @pl.when(ph == 0)
    def _():
        z = zb[...]
        zr = jnp.dot(c_ref[rows, :], z, preferred_element_type=F32)
        zi = jnp.dot(s_ref[0, rows, :], z, preferred_element_type=F32)
        kr, ki = kr_ref[0], ki_ref[0]
        first = (lax.broadcasted_iota(jnp.int32, zr.shape, 0) == 0) & (blk == 0)
        pr = kr * zr - jnp.where(first, 0.0, ki * zi)
        pi = jnp.where(first, kn_ref[0] * zi, kr * zi + ki * zr)
        pre[rows, :] = pr.astype(BF16)
        pim[rows, :] = pi.astype(BF16)

    @pl.when(ph == 1)
    def _():
        y = (jnp.dot(c_ref[rows, :], pre[...], preferred_element_type=F32)
             + jnp.dot(s_ref[1, rows, :], pim[...], preferred_element_type=F32))
        zin = zf[rows, :]
        gate = jnp.where(o == 0, x1_ref[rows, :], x2_ref[rows, :])
        zn = gate * (y + d_ref[0] * zin)
        zf[rows, :] = zn
        zb[rows, :] = zn.astype(BF16)

        @pl.when(o == order - 1)
        def _():
            o_ref[rows, :] = zn


def _hy_conv(v, x1, x2, cb, s12, kr, ki, kn, d_hy, hy_w):
    L, BC = v.shape
    order = kr.shape[0]
    nc = 256
    rb = min(1024, L)
    nblk = L // rb
    nct = hy_w // nc
    col = pl.BlockSpec((L, nc), lambda c, o, p, k: (0, c))
    kmap = lambda c, o, p, k: (o, k * (1 - p) + (nblk - 1) * p, c % nct)
    return pl.pallas_call(
        functools.partial(_hy_conv_kernel, rb=rb, order=order),
        grid=(BC // nc, order, 2, nblk),
        in_specs=[col, col, col,
                  pl.BlockSpec((L, L), lambda c, o, p, k: (0, 0), pipeline_mode=pl.Buffered(1)),
                  pl.BlockSpec((2, L, L), lambda c, o, p, k: (0, 0, 0), pipeline_mode=pl.Buffered(1)),
                  pl.BlockSpec((1, rb, nc), kmap),
                  pl.BlockSpec((1, rb, nc), kmap),
                  pl.BlockSpec((1, 1, nc), lambda c, o, p, k: (o, 0, c % nct)),
                  pl.BlockSpec((1, 1, nc), lambda c, o, p, k: (o, 0, c % nct))],
        out_specs=col,
        out_shape=jax.ShapeDtypeStruct((L, BC), F32),
        scratch_shapes=[pltpu.VMEM((L, nc), F32), pltpu.VMEM((L, nc), BF16),
                        pltpu.VMEM((L, nc), BF16), pltpu.VMEM((L, nc), BF16)],
        compiler_params=_cparams(("arbitrary",) * 4),
        name="hyena_conv",
    )(v, x1, x2, cb, s12, kr, ki, kn, d_hy.reshape(order, 1, hy_w))


def _merge_kernel(y_ref, z_ref, x_ref, mod_ref, wg_ref, bg_ref, g5_ref, gh_ref, wo5_ref, woh_ref, g2_ref,
                  x1_ref, h2t_ref):
    y = y_ref[0] + y_ref[1]
    gy = _gelu(y)
    gate = jnp.dot(gy.astype(BF16), wg_ref[...], preferred_element_type=F32) + bg_ref[...]
    s5 = gy * _sigmoid(gate)
    n5 = _rms(s5, g5_ref[...])
    nh = _rms(z_ref[...], gh_ref[...])
    mix = (jnp.dot(n5.astype(BF16), wo5_ref[...], preferred_element_type=F32)
           + jnp.dot(nh.astype(BF16), woh_ref[...], preferred_element_type=F32))
    x1 = x_ref[0] + mod_ref[0, 2:3, :] * mix
    x1_ref[0] = x1
    h2 = _rms(x1, g2_ref[...]) * (1.0 + mod_ref[0, 4:5, :]) + mod_ref[0, 3:4, :]
    h2t_ref[0] = h2.T.astype(BF16)


def _merge(y, z, x, mod3, w_glu, b_glu, g5, gh, w_out, g2):
    B, L, D = x.shape
    W5 = w_glu.shape[0]
    WH = z.shape[1] // B
    tl = min(512, L)
    full = lambda shp: pl.BlockSpec(shp, lambda b_, t: (0,) * len(shp))
    return pl.pallas_call(
        _merge_kernel,
        grid=(B, L // tl),
        in_specs=[pl.BlockSpec((2, tl, W5), lambda b_, t: (0, t, b_)),
                  pl.BlockSpec((tl, WH), lambda b_, t: (t, b_)),
                  pl.BlockSpec((1, tl, D), lambda b_, t: (b_, t, 0)),
                  pl.BlockSpec((1, N_MOD, D), lambda b_, t: (b_, 0, 0)),
                  full((W5, W5)), full((1, W5)), full((1, W5)), full((1, WH)),
                  full((W5, D)), full((WH, D)), full((1, D))],
        out_specs=[pl.BlockSpec((1, tl, D), lambda b_, t: (b_, t, 0)),
                   pl.BlockSpec((1, D, tl), lambda b_, t: (b_, 0, t))],
        out_shape=[jax.ShapeDtypeStruct((B, L, D), F32), jax.ShapeDtypeStruct((B, D, L), BF16)],
        compiler_params=_cparams(("arbitrary", "arbitrary")),
        name="merge_heads",
    )(y, z, x, mod3, w_glu.astype(BF16), b_glu.reshape(1, W5), g5.reshape(1, W5), gh.reshape(1, WH),
      w_out[:W5].astype(BF16), w_out[W5:].astype(BF16), g2.reshape(1, D))


def _topk_rank(s, k):
    vals = []
    cur = s
    rank = jnp.full(s.shape, float(k), F32)
    for i in range(k):
        m = jnp.max(cur, axis=0, keepdims=True)
        hit = cur == m
        vals.append(m)
        cur = jnp.where(hit, -jnp.inf, cur)
        rank = jnp.where(hit, float(i), rank)
    return jnp.concatenate(vals, axis=0), rank


def _topk_desc(s, k):
    vals = []
    cur = s
    for _ in range(k):
        m = jnp.max(cur, axis=0, keepdims=True)
        vals.append(m)
        cur = jnp.where(cur == m, -jnp.inf, cur)
    return jnp.concatenate(vals, axis=0)


def _gelu_packed(x):
    c = math.sqrt(2.0 / math.pi)
    inner = x * (c + (c * 0.044715) * (x * x))
    hx = 0.5 * x
    return hx + hx * jnp.tanh(inner)


PACK = 16
LANES = 128
SUBLANES = 8


def _packed_row(ref, lead, row, n_tiles):
    r8 = jnp.concatenate([ref[lead + (c, pl.ds(row, SUBLANES, stride=0), slice(None))]
                          for c in range(n_tiles)], axis=1)
    return jnp.concatenate([r8, r8], axis=0).astype(BF16)


def _peer_kernel(h_ref, x1_ref, mod_ref, wq_ref, k1_ref, k2_ref, u_ref, vt_ref, gf_ref, o_ref,
                 ea, cs, eb, r2b, acc, a0, a1, g0, g1, *, nb1, ne, topk):
    f = pl.program_id(0)
    total = pl.num_programs(0) - 2
    H, NK, _ = k1_ref.shape
    T = h_ref.shape[2]
    f2 = jnp.clip(f - 1, 0, total - 1)
    f3 = jnp.clip(f - 2, 0, total - 1)
    par2 = (f2 // ne) % 2
    par3 = (f3 // ne) % 2
    e_mid = f2 % ne

    @pl.when(f == 0)
    def _():
        for buf in (a0, a1, g0, g1):
            buf[...] = jnp.zeros_like(buf)

    @pl.when((f % ne == 0) & (f < total))
    def _():
        par = (f // ne) % 2
        acc[par] = jnp.zeros(acc.shape[1:], F32)
        qt = jnp.dot(wq_ref[...], h_ref[0], preferred_element_type=F32)
        dk = k1_ref.shape[2]
        for hh in range(H):
            q1 = qt[(2 * hh) * dk:(2 * hh + 1) * dk, :]
            q2 = qt[(2 * hh + 1) * dk:(2 * hh + 2) * dk, :]
            s1f = jnp.dot(k1_ref[hh], q1.astype(BF16), preferred_element_type=F32)
            s2f = jnp.dot(k2_ref[hh], q2.astype(BF16), preferred_element_type=F32)
            for c0 in range(0, T, LANES):
                lc = slice(c0, c0 + LANES)
                s1, s2 = s1f[:, lc], s2f[:, lc]
                v1 = _topk_desc(s1, topk)
                v2, r2 = _topk_rank(s2, topk)
                cands = [v1[a:a + 1] + v2[b:b + 1]
                         for a in range(topk) for b in range(topk) if (a + 1) * (b + 1) <= topk]
                sel = _topk_desc(jnp.concatenate(cands, axis=0), topk)
                tau = sel[topk - 1:topk]
                mx = sel[0:1]
                rz = 1.0 / jnp.sum(jnp.exp(sel - mx), axis=0, keepdims=True)
                cnt = jnp.zeros((NK, LANES), F32)
                for a_ in range(topk):
                    c_a = jnp.sum(jnp.where(v1[a_:a_ + 1] + v2 >= tau, 1.0, 0.0), axis=0, keepdims=True)
                    cnt = jnp.where(s1 == v1[a_:a_ + 1], c_a, cnt)
                ea[par, hh, c0 // LANES] = jnp.exp(s1 - v1[0:1]) * rz
                cs[par, hh, c0 // LANES] = cnt
                eb[par, hh, :, lc] = jnp.exp(s2 - v2[0:1]).astype(BF16)
                r2b[par, hh, :, lc] = r2.astype(BF16)

    def stages(a_new, a_old, g_new, g_old):
        th = T // 2
        neb = nb1 * NK
        pieces = []
        for half in range(2):
            tc = slice(half * th, (half + 1) * th)
            for m0 in range(half * neb // 2, (half + 1) * neb // 2, MXU_ROWS):
                pieces.append(("a", slice(m0, m0 + MXU_ROWS), slice(None)))
            for m0 in range(0, vt_ref.shape[0], MXU_ROWS):
                pieces.append(("acc", slice(m0, m0 + MXU_ROWS), tc))

        def mxu_piece(kind, ms, tc):
            if kind == "a":
                a_new[ms, tc] = jnp.dot(u_ref[ms, :], h_ref[0, :, tc], preferred_element_type=F32)
            else:
                acc[par3, ms, tc] += jnp.dot(vt_ref[ms, :], g_old[:, tc], preferred_element_type=F32)

        def vpu_unit(j, r_lo, r_hi):
            i1 = e_mid * nb1 + j
            rss = [slice(r * PACK, (r + 1) * PACK) for r in range(r_lo, r_hi)]
            ws = [jnp.zeros((PACK, T), BF16) for _ in rss]
            for hh in range(H):
                ea_p = _packed_row(ea, (par2, hh), i1, T // LANES)
                cs_p = _packed_row(cs, (par2, hh), i1, T // LANES)
                for k, rs in enumerate(rss):
                    ws[k] = ws[k] + jnp.where(r2b[par2, hh, rs, :] < cs_p, ea_p * eb[par2, hh, rs, :],
                                              jnp.zeros((), BF16))
            for k, rs in enumerate(rss):
                rows = slice(j * NK + rs.start, j * NK + rs.stop)
                g_new[rows, :] = _gelu_packed(a_old[rows, :].astype(BF16)) * ws[k]

        nr = NK // PACK
        units = [(j, r0, r0 + VPU_UNIT) for j in range(nb1) for r0 in range(0, nr, VPU_UNIT)]
        weight = [u_ref.shape[1] if kind == "a" else neb for kind, _, _ in pieces]
        done = 0
        for k, piece in enumerate(pieces):
            mxu_piece(*piece)
            upto = len(units) * sum(weight[:k + 1]) // sum(weight)
            for unit in units[done:upto]:
                vpu_unit(*unit)
            done = upto

    @pl.when(f % 2 == 0)
    def _():
        stages(a0, a1, g1, g0)

    @pl.when(f % 2 == 1)
    def _():
        stages(a1, a0, g0, g1)

    @pl.when((f >= 2) & ((f - 1) % ne == 0))
    def _():
        x2 = x1_ref[0] + mod_ref[0, 5:6, :] * acc[par3].T
        o_ref[0] = _rms(x2, gf_ref[...])


def _peer(h2t, x1, mod3, wq, k1, k2, u_tab, v_tab, g_final):
    B, L, D = x1.shape
    H, NK, dk = k1.shape
    T = min(512, L)
    nb1 = 8
    nlt = L // T
    ne = NK // nb1
    total = B * nlt * ne
    t_in = lambda f: jnp.minimum(f // ne, B * nlt - 1)
    t_out = lambda f: jnp.maximum(f - 2, 0) // ne
    once = lambda shp: pl.BlockSpec(shp, lambda f: (0,) * len(shp), pipeline_mode=pl.Buffered(1))
    tok = pl.BlockSpec((1, T, D), lambda f: (t_out(f) // nlt, t_out(f) % nlt, 0))
    return pl.pallas_call(
        functools.partial(_peer_kernel, nb1=nb1, ne=ne, topk=PEER_TOPK),
        grid=(total + 2,),
        in_specs=[pl.BlockSpec((1, D, T), lambda f: (t_in(f) // nlt, 0, t_in(f) % nlt)), tok,
                  pl.BlockSpec((1, N_MOD, D), lambda f: (t_out(f) // nlt, 0, 0)),
                  once(wq.shape[::-1]), once(k1.shape), once(k2.shape),
                  pl.BlockSpec((nb1 * NK, D), lambda f: (jnp.minimum(f, total - 1) % ne, 0)),
                  pl.BlockSpec((D, nb1 * NK), lambda f: (0, jnp.maximum(f - 2, 0) % ne)),
                  once((1, D))],
        out_specs=tok,
        out_shape=jax.ShapeDtypeStruct((B, L, D), F32),
        scratch_shapes=[pltpu.VMEM((2, H, T // LANES, NK, LANES), F32),
                        pltpu.VMEM((2, H, T // LANES, NK, LANES), F32),
                        pltpu.VMEM((2, H, NK, T), BF16), pltpu.VMEM((2, H, NK, T), BF16),
                        pltpu.VMEM((2, D, T), F32),
                        pltpu.VMEM((nb1 * NK, T), F32), pltpu.VMEM((nb1 * NK, T), F32),
                        pltpu.VMEM((nb1 * NK, T), BF16), pltpu.VMEM((nb1 * NK, T), BF16)],
        compiler_params=_cparams(("arbitrary",)),
        name="peer_dense",
    )(h2t, x1, mod3, wq.astype(BF16).T, k1.astype(BF16), k2.astype(BF16), u_tab.astype(BF16),
      v_tab.astype(BF16).T, g_final.reshape(1, D))


def kernel(x, c, ctx, c_ctx, w_ada, b_ada, g_norm1, g_norm2, w_in, b_in, s5_a_re, s5_a_im, s5_log_step, s5_b_re, s5_b_im, s5_c_re, s5_c_im, s5_d, w_glu, b_glu, hy_conv_w, hy_conv_b, hf_w1, hf_b1, hf_wh, hf_bh, hf_freq, hf_wout, hf_decay, hy_d, g_out_s5, g_out_hy, w_out, peer_wq, peer_k1, peer_k2, peer_u, peer_v, g_final):
    B, L, D = x.shape
    Lc = ctx.shape[1]
    depth = w_ada.shape[0]
    assert depth == 1, "single-layer block"
    l = 0
    S5W = w_glu.shape[1]
    HYW = g_out_hy.shape[1]
    order = hy_d.shape[1]
    _, _, G, P, Hs = s5_b_re.shape

    rpad = (-(B + 1)) % 8
    cc = jnp.concatenate([c, c_ctx[None], jnp.zeros((rpad, D), F32)], axis=0)
    mod = _ada(cc, w_ada[l], b_ada[l])
    mod_x = mod[:B].reshape(B, N_MOD, D)
    mod_c = mod[B:B + 1].reshape(1, N_MOD, D)

    w_in_bf = w_in[l].astype(BF16)
    u_x, v, xg1, xg2 = _inproj(x, mod_x, g_norm1[l], w_in_bf, b_in[l], hy_conv_w[l], hy_conv_b[l],
                               s5_w=S5W, hy_w=HYW, hyena=True)
    (u_c,) = _inproj(ctx, mod_c, g_norm1[l], w_in_bf[:, :S5W], b_in[l][:S5W], None, None,
                     s5_w=S5W, hy_w=HYW, hyena=False)

    lr, li, bbr, bbi = _s5_prep(s5_a_re[l], s5_a_im[l], s5_log_step[l], s5_b_re[l], s5_b_im[l])
    gin = MXU_W // Hs
    lam = jnp.stack([lr.reshape(2, G * P), li.reshape(2, G * P)], axis=1)
    tr = lambda m: jnp.swapaxes(m, 2, 3)
    wb = jnp.stack([_block_diag(tr(bbr), gin), _block_diag(tr(bbi), gin)], axis=1).astype(BF16)
    cm = jnp.stack([_block_diag(tr(s5_c_re[l]), gin), _block_diag(-tr(s5_c_im[l]), gin)], axis=1).astype(BF16)
    h_zero = jnp.zeros((2, 2, B, G * P), F32)
    _, h_ctx = _s5_scan(u_c.reshape(Lc, B, S5W), h_zero, lam, wb, cm, s5_d[l], emit_y=False)
    y5, _ = _s5_scan(u_x.reshape(L, B, S5W), h_ctx, lam, wb, cm, s5_d[l], emit_y=True)

    hs, hd = _hy_filters(L, hf_w1[l], hf_b1[l], hf_wh[l], hf_bh[l], hf_freq[l], hf_wout[l], hf_decay[l],
                         HYW, order)
    c32, s32, cb, s1b, s2b = _dft_mats(L)
    kr, ki, kn = _hy_spectra(c32, s32, hs, hd, HYW, order)
    z_hy = _hy_conv(v, xg1, xg2, cb, jnp.stack([s1b, s2b]), kr, ki, kn, hy_d[l], HYW)

    x1, h2 = _merge(y5.reshape(2, L, B * S5W), z_hy, x, mod_x, w_glu[l], b_glu[l],
                    g_out_s5[l], g_out_hy[l], w_out[l], g_norm2[l])
    return _peer(h2, x1, mod_x, peer_wq[l], peer_k1[l], peer_k2[l], peer_u[l], peer_v[l], g_final)
```

```python
import functools
import math

import jax
import jax.numpy as jnp
from jax import lax
from jax.experimental import pallas as pl
from jax.experimental.pallas import tpu as pltpu

F32 = jnp.float32
BF16 = jnp.bfloat16
HIGHEST = lax.Precision.HIGHEST

EPS = 1e-6
GRID_W = 64
PEER_TOPK = 16
N_MOD = 6

MXU_W = 256
VMEM_LIMIT = 56 * 1024 * 1024

ADA_COLS = 1024
TOKEN_ROWS = 512
S5_STEPS = 32
S5_COLS = 256
DFT_ROWS = 256
HY_COLS = 256
HY_ROWS = 1024
PEER_TOKENS = 512
PEER_BLOCKS = 8
MXU_ROWS = 512
VPU_UNIT = 8


def _cparams(sem):
    return pltpu.CompilerParams(dimension_semantics=sem, vmem_limit_bytes=VMEM_LIMIT)


def _gelu(x):
    c = math.sqrt(2.0 / math.pi)
    return 0.5 * x * (1.0 + jnp.tanh(c * (x + 0.044715 * (x * x * x))))


def _sigmoid(x):
    return 1.0 / (1.0 + jnp.exp(-x))


def _rms(x, g):
    return x * lax.rsqrt(jnp.mean(x * x, axis=-1, keepdims=True) + EPS) * g


def _ada_kernel(c_ref, w_ref, b_ref, o_ref):
    c = c_ref[...]
    s = c * _sigmoid(c)
    o_ref[...] = jnp.dot(s, w_ref[...], preferred_element_type=F32, precision=HIGHEST) + b_ref[...]


def _ada(cc, w, b):
    R, D = cc.shape
    N = w.shape[1]
    tn = ADA_COLS
    return pl.pallas_call(
        _ada_kernel,
        grid=(N // tn,),
        in_specs=[pl.BlockSpec((R, D), lambda j: (0, 0)),
                  pl.BlockSpec((D, tn), lambda j: (0, j)),
                  pl.BlockSpec((1, tn), lambda j: (0, j))],
        out_specs=pl.BlockSpec((R, tn), lambda j: (0, j)),
        out_shape=jax.ShapeDtypeStruct((R, N), F32),
        compiler_params=_cparams(("arbitrary",)),
        name="ada_mod",
    )(cc, w, b.reshape(1, N))


def _inproj_kernel(x_ref, mod_ref, g_ref, w_ref, b_ref, *rest, s5_w, hy_w, hyena):
    x = x_ref[0]
    h = _rms(x, g_ref[...])
    h = h * (1.0 + mod_ref[0, 1:2, :]) + mod_ref[0, 0:1, :]
    p = jnp.dot(h.astype(BF16), w_ref[...], preferred_element_type=F32) + b_ref[...]
    if not hyena:
        (u_ref,) = rest
        u_ref[...] = p
        return
    cw_ref, cb_ref, u_ref, v_ref, x1_ref, x2_ref = rest
    u_ref[...] = p[:, :s5_w]
    ph = p[:, s5_w:]
    tl = ph.shape[0]
    pos = lax.broadcasted_iota(jnp.int32, ph.shape, 0) % GRID_W
    prev = jnp.where(pos == 0, 0.0, pltpu.roll(ph, 1, axis=0))
    nxt = jnp.where(pos == GRID_W - 1, 0.0, pltpu.roll(ph, tl - 1, axis=0))
    q = cb_ref[...] + cw_ref[0:1, :] * prev + cw_ref[1:2, :] * ph + cw_ref[2:3, :] * nxt
    v_ref[...] = q[:, :hy_w]
    x1_ref[...] = q[:, hy_w:2 * hy_w]
    x2_ref[...] = q[:, 2 * hy_w:]


def _inproj(x, mod3, g, w_bf, b, conv_w, conv_b, *, s5_w, hy_w, hyena):
    B, L, D = x.shape
    N = w_bf.shape[1]
    tl = min(TOKEN_ROWS, L)
    nb = mod3.shape[0]
    mod_map = (lambda b_, t: (b_, 0, 0)) if nb > 1 else (lambda b_, t: (0, 0, 0))
    in_specs = [pl.BlockSpec((1, tl, D), lambda b_, t: (b_, t, 0)),
                pl.BlockSpec((1, N_MOD, D), mod_map),
                pl.BlockSpec((1, D), lambda b_, t: (0, 0)),
                pl.BlockSpec((D, N), lambda b_, t: (0, 0)),
                pl.BlockSpec((1, N), lambda b_, t: (0, 0))]
    args = [x, mod3, g.reshape(1, D), w_bf, b.reshape(1, N)]
    out_specs = [pl.BlockSpec((tl, s5_w), lambda b_, t: (t, b_))]
    out_shape = [jax.ShapeDtypeStruct((L, B * s5_w), F32)]
    if hyena:
        nh = N - s5_w
        in_specs += [pl.BlockSpec((3, nh), lambda b_, t: (0, 0)),
                     pl.BlockSpec((1, nh), lambda b_, t: (0, 0))]
        args += [conv_w, conv_b.reshape(1, nh)]
        out_specs += [pl.BlockSpec((tl, hy_w), lambda b_, t: (t, b_))] * 3
        out_shape += [jax.ShapeDtypeStruct((L, B * hy_w), F32)] * 3
    return pl.pallas_call(
        functools.partial(_inproj_kernel, s5_w=s5_w, hy_w=hy_w, hyena=hyena),
        grid=(B, L // tl),
        in_specs=in_specs, out_specs=out_specs, out_shape=out_shape,
        compiler_params=_cparams(("arbitrary", "arbitrary")),
        name="inproj_hy" if hyena else "inproj_ctx",
    )(*args)


def _s5_prep_kernel(are_ref, aim_ref, ls_ref, bre_ref, bim_ref, lr_ref, li_ref, bbr_ref, bbi_ref):
    a_re, a_im = are_ref[...], aim_ref[...]
    step = jnp.exp(ls_ref[...])
    er = jnp.exp(a_re * step)
    lr = er * jnp.cos(a_im * step)
    li = er * jnp.sin(a_im * step)
    nr, ni = lr - 1.0, li
    d2 = a_re * a_re + a_im * a_im
    kr = (nr * a_re + ni * a_im) / d2
    ki = (ni * a_re - nr * a_im) / d2
    br, bi = bre_ref[...], bim_ref[...]
    lr_ref[...] = lr
    li_ref[...] = li
    bbr_ref[...] = kr * br - ki * bi
    bbi_ref[...] = kr * bi + ki * br


def _s5_prep(a_re, a_im, log_step, b_re, b_im):
    two, G, P, H = b_re.shape
    n = two * G * P
    col = lambda a: a.reshape(n, 1)
    ls = jnp.broadcast_to(log_step[:, :, None], (two, G, P))
    outs = pl.pallas_call(
        _s5_prep_kernel,
        out_shape=[jax.ShapeDtypeStruct((n, 1), F32)] * 2 + [jax.ShapeDtypeStruct((n, H), F32)] * 2,
        name="s5_prep",
    )(col(a_re), col(a_im), col(ls), b_re.reshape(n, H), b_im.reshape(n, H))
    lr, li, bbr, bbi = outs
    return (lr.reshape(two, G, P), li.reshape(two, G, P),
            bbr.reshape(two, G, P, H), bbi.reshape(two, G, P, H))


def _block_diag(m, gb):
    two, G, R, C = m.shape
    m = m.reshape(two, G // gb, gb, R, C)
    eye = jnp.eye(gb, dtype=m.dtype)
    out = m[:, :, :, :, None, :] * eye[None, None, :, None, :, None]
    return out.reshape(two, G // gb, gb * R, gb * C)


def _s5_kernel(u_ref, h0_ref, lam_ref, wb_ref, cm_ref, d_ref, *rest, tl, cw, emit_y):
    if emit_y:
        y_ref, hfin_ref, sre, sim, hre, him = rest
    else:
        hfin_ref, sre, sim, hre, him = rest
    dr = pl.program_id(0)
    i = pl.program_id(1)
    B = u_ref.shape[1]
    SW = sre.shape[1]
    nsb = wb_ref.shape[2]
    kin = wb_ref.shape[3]
    kst = wb_ref.shape[4]

    @pl.when(i == 0)
    def _():
        hre[...] = h0_ref[0, 0]
        him[...] = h0_ref[0, 1]

    u = u_ref[...].reshape(tl * B, u_ref.shape[2])
    ub = u.astype(BF16)
    for j in range(nsb):
        uj = ub[:, j * kin:(j + 1) * kin]
        sre[:, j * kst:(j + 1) * kst] = jnp.dot(uj, wb_ref[0, 0, j], preferred_element_type=F32)
        sim[:, j * kst:(j + 1) * kst] = jnp.dot(uj, wb_ref[0, 1, j], preferred_element_type=F32)

    for c in range(SW // cw):
        cols = slice(c * cw, (c + 1) * cw)
        lr = jnp.broadcast_to(lam_ref[0, 0:1, cols], (B, cw))
        li = jnp.broadcast_to(lam_ref[0, 1:2, cols], (B, cw))

        def body(s, carry, cols=cols, lr=lr, li=li):
            h_r, h_i = carry
            t = jnp.where(dr == 0, s, tl - 1 - s)
            row = pl.multiple_of(t * B, B)
            n_r = lr * h_r - li * h_i + sre[pl.ds(row, B), cols]
            n_i = lr * h_i + li * h_r + sim[pl.ds(row, B), cols]
            sre[pl.ds(row, B), cols] = n_r
            sim[pl.ds(row, B), cols] = n_i
            return n_r, n_i

        h_r, h_i = lax.fori_loop(0, tl, body, (hre[:, cols], him[:, cols]), unroll=2)
        hre[:, cols] = h_r
        him[:, cols] = h_i

    hfin_ref[0, 0] = hre[...]
    hfin_ref[0, 1] = him[...]

    if emit_y:
        kout = cm_ref.shape[4]
        dsel = d_ref[...] * jnp.where(dr == 0, 1.0, 0.0)
        for n in range(nsb):
            y = (jnp.dot(sre[:, n * kst:(n + 1) * kst].astype(BF16), cm_ref[0, 0, n], preferred_element_type=F32)
                 + jnp.dot(sim[:, n * kst:(n + 1) * kst].astype(BF16), cm_ref[0, 1, n], preferred_element_type=F32))
            oc = slice(n * kout, (n + 1) * kout)
            y = y + dsel[:, oc] * u[:, oc]
            y_ref[0, :, :, oc] = y.reshape(tl, B, kout).astype(y_ref.dtype)


def _s5_scan(u3, h0, lam, wb, cm, d, *, emit_y):
    L, B, W = u3.shape
    SW = lam.shape[2]
    tl = min(S5_STEPS, L)
    nT = L // tl
    tmap = lambda dr, i: i + dr * (nT - 1 - 2 * i)
    in_specs = [pl.BlockSpec((tl, B, W), lambda dr, i: (tmap(dr, i), 0, 0)),
                pl.BlockSpec((1, 2, B, SW), lambda dr, i: (dr, 0, 0, 0)),
                pl.BlockSpec((1, 2, SW), lambda dr, i: (dr, 0, 0)),
                pl.BlockSpec((1,) + wb.shape[1:], lambda dr, i: (dr, 0, 0, 0, 0)),
                pl.BlockSpec((1,) + cm.shape[1:], lambda dr, i: (dr, 0, 0, 0, 0)),
                pl.BlockSpec((1, W), lambda dr, i: (0, 0))]
    out_specs = [pl.BlockSpec((1, 2, B, SW), lambda dr, i: (dr, 0, 0, 0))]
    out_shape = [jax.ShapeDtypeStruct((2, 2, B, SW), F32)]
    if emit_y:
        out_specs = [pl.BlockSpec((1, tl, B, W), lambda dr, i: (dr, tmap(dr, i), 0, 0))] + out_specs
        out_shape = [jax.ShapeDtypeStruct((2, L, B, W), BF16)] + out_shape
    res = pl.pallas_call(
        functools.partial(_s5_kernel, tl=tl, cw=S5_COLS, emit_y=emit_y),
        grid=(2, nT),
        in_specs=in_specs, out_specs=out_specs, out_shape=out_shape,
        scratch_shapes=[pltpu.VMEM((tl * B, SW), F32), pltpu.VMEM((tl * B, SW), F32),
                        pltpu.VMEM((B, SW), F32), pltpu.VMEM((B, SW), F32)],
        compiler_params=_cparams(("arbitrary", "arbitrary")),
        name="s5_scan_x" if emit_y else "s5_scan_ctx",
    )(u3, h0, lam, wb, cm, d.reshape(1, W))
    return res if emit_y else (None, res[0])


def _hy_filter_kernel(w1t_ref, w1c_ref, w1s_ref, b1_ref, wh_ref, bh_ref, fr_ref, wout_ref, dec_ref,
                      bands_ref, hs_ref, hd_ref, *, L, hy_w):
    pos = lax.broadcasted_iota(jnp.int32, (L, 1), 0).astype(F32)
    t = pos / max(L - 1, 1)
    ang = (2.0 * math.pi / L) * pos * bands_ref[...]
    fr = fr_ref[...]
    dot = functools.partial(jnp.dot, preferred_element_type=F32, precision=HIGHEST)
    pre = t * w1t_ref[...] + dot(jnp.cos(ang), w1c_ref[...]) + dot(-jnp.sin(ang), w1s_ref[...]) + b1_ref[...]
    hid = jnp.sin(fr * pre)
    for i in range(wh_ref.shape[0]):
        hid = jnp.sin(fr * (dot(hid, wh_ref[i]) + bh_ref[i]))
    h = dot(hid, wout_ref[...]) * jnp.exp(-t * jnp.abs(dec_ref[...]))
    nrm = jnp.sum(h * h, axis=0, keepdims=True)
    nrm = nrm[:, :hy_w] + nrm[:, hy_w:]
    scale = lax.rsqrt(nrm + EPS)
    hf = h[:, :hy_w] * scale
    hb = h[:, hy_w:] * scale
    hb0 = jnp.where(lax.broadcasted_iota(jnp.int32, hb.shape, 0) == 0, 0.0, hb)
    hs_ref[...] = hf + hb0
    hd_ref[...] = hf - hb0


def _hy_filters(L, w1, b1, wh, bh, freq, wout, decay, hy_w, order):
    nb = (w1.shape[0] - 1) // 2
    hid = w1.shape[1]
    bands = jnp.linspace(1e-4, nb - 1, nb, dtype=F32).reshape(1, nb)
    full = lambda shp: pl.BlockSpec(shp, lambda o: (0,) * len(shp))
    return pl.pallas_call(
        functools.partial(_hy_filter_kernel, L=L, hy_w=hy_w),
        grid=(order,),
        in_specs=[full((1, hid)), full((nb, hid)), full((nb, hid)), full((1, hid)),
                  full(wh.shape), full((wh.shape[0], 1, hid)), full((1, hid)),
                  pl.BlockSpec((hid, 2 * hy_w), lambda o: (0, o)),
                  pl.BlockSpec((1, 2 * hy_w), lambda o: (0, o)),
                  full((1, nb))],
        out_specs=[pl.BlockSpec((L, hy_w), lambda o: (0, o))] * 2,
        out_shape=[jax.ShapeDtypeStruct((L, order * hy_w), F32)] * 2,
        compiler_params=_cparams(("arbitrary",)),
        name="hyena_filters",
    )(w1[0:1], w1[1:1 + nb], w1[1 + nb:], b1.reshape(1, hid), wh, bh.reshape(wh.shape[0], 1, hid),
      freq.reshape(1, hid), wout, decay.reshape(1, -1), bands)


def _dft_kernel(c32_ref, s32_ref, cb_ref, s1_ref, s2_ref, *, L, rb):
    i = pl.program_id(0)
    f = lax.broadcasted_iota(jnp.int32, (rb, L), 0) + i * rb
    t = lax.broadcasted_iota(jnp.int32, (rb, L), 1)
    k = (f * t) % (2 * L)
    ang = k.astype(F32) * (math.pi / L)
    c = jnp.cos(ang)
    s = -jnp.sin(ang)
    c32_ref[...] = c
    s32_ref[...] = s
    cb_ref[...] = c.astype(BF16)
    sign_t = jnp.where(t % 2 == 0, 1.0, -1.0)
    sign_f = jnp.where(f % 2 == 0, 1.0, -1.0)
    s1_ref[...] = jnp.where(f == 0, sign_t, s).astype(BF16)
    s2_ref[...] = jnp.where(t == 0, sign_f, s).astype(BF16)


def _dft_mats(L):
    rb = min(DFT_ROWS, L)
    spec = pl.BlockSpec((rb, L), lambda i: (i, 0))
    return pl.pallas_call(
        functools.partial(_dft_kernel, L=L, rb=rb),
        grid=(L // rb,),
        in_specs=[],
        out_specs=[spec] * 5,
        out_shape=[jax.ShapeDtypeStruct((L, L), F32)] * 2 + [jax.ShapeDtypeStruct((L, L), BF16)] * 3,
        compiler_params=_cparams(("arbitrary",)),
        name="dft_mats",
    )()


def _spec_kernel(c_ref, s_ref, hs_ref, hd_ref, kr_ref, ki_ref, kn_ref, *, L, rb):
    i = pl.program_id(1)
    f = lax.broadcasted_iota(jnp.int32, (rb, 1), 0) + i * rb
    w = jnp.where(f == 0, 1.0, 2.0) * (1.0 / (2 * L))
    dot = functools.partial(jnp.dot, preferred_element_type=F32, precision=HIGHEST)
    hs = hs_ref[...]
    kr_ref[0] = w * dot(c_ref[...], hs)
    ki_ref[0] = w * dot(s_ref[...], hd_ref[...])
    sign = jnp.where(lax.broadcasted_iota(jnp.int32, (L, 1), 0) % 2 == 0, 1.0, -1.0)
    kn_ref[0] = jnp.sum(sign * hs, axis=0, keepdims=True) * (1.0 / (2 * L))


def _hy_spectra(c32, s32, hs, hd, hy_w, order):
    L = c32.shape[0]
    rb = min(DFT_ROWS, L)
    mat = pl.BlockSpec((rb, L), lambda o, i: (i, 0))
    hsp = pl.BlockSpec((L, hy_w), lambda o, i: (0, o))
    return pl.pallas_call(
        functools.partial(_spec_kernel, L=L, rb=rb),
        grid=(order, L // rb),
        in_specs=[mat, mat, hsp, hsp],
        out_specs=[pl.BlockSpec((1, rb, hy_w), lambda o, i: (o, i, 0))] * 2
                  + [pl.BlockSpec((1, 1, hy_w), lambda o, i: (o, 0, 0))],
        out_shape=[jax.ShapeDtypeStruct((order, L, hy_w), F32)] * 2
                  + [jax.ShapeDtypeStruct((order, 1, hy_w), F32)],
        compiler_params=_cparams(("arbitrary", "arbitrary")),
        name="hyena_spectra",
    )(c32, s32, hs, hd)


def _hy_conv_kernel(v_ref, x1_ref, x2_ref, c_ref, s_ref, kr_ref, ki_ref, kn_ref, d_ref, o_ref,
                    zf, zb, pre, pim, *, rb, order):
    o = pl.program_id(1)
    ph = pl.program_id(2)
    blk = pl.program_id(3)
    rows = pl.ds(pl.multiple_of(blk * rb, rb), rb)

    @pl.when((o == 0) & (ph == 0) & (blk == 0))
    def _():
        v = v_ref[...]
        zf[...] = v
        zb[...] = v.astype(BF16)

    @pl.when(ph == 0)
    def _():
        z = zb[...]
        zr = jnp.dot(c_ref[rows, :], z, preferred_element_type=F32)
        zi = jnp.dot(s_ref[0, rows, :], z, preferred_element_type=F32)
        kr, ki = kr_ref[0], ki_ref[0]
        first = (lax.broadcasted_iota(jnp.int32, zr.shape, 0) == 0) & (blk == 0)
        pr = kr * zr - jnp.where(first, 0.0, ki * zi)
        pi = jnp.where(first, kn_ref[0] * zi, kr * zi + ki * zr)
        pre[rows, :] = pr.astype(BF16)
        pim[rows, :] = pi.astype(BF16)

    @pl.when(ph == 1)
    def _():
        y = (jnp.dot(c_ref[rows, :], pre[...], preferred_element_type=F32)
             + jnp.dot(s_ref[1, rows, :], pim[...], preferred_element_type=F32))
        zin = zf[rows, :]
        gate = jnp.where(o == 0, x1_ref[rows, :], x2_ref[rows, :])
        zn = gate * (y + d_ref[0] * zin)
        zf[rows, :] = zn
        zb[rows, :] = zn.astype(BF16)

        @pl.when(o == order - 1)
        def _():
            o_ref[rows, :] = zn


def _hy_conv(v, x1, x2, cb, s12, kr, ki, kn, d_hy, hy_w):
    L, BC = v.shape
    order = kr.shape[0]
    nc = HY_COLS
    rb = min(HY_ROWS, L)
    nblk = L // rb
    nct = hy_w // nc
    col = pl.BlockSpec((L, nc), lambda c, o, p, k: (0, c))
    kmap = lambda c, o, p, k: (o, k * (1 - p) + (nblk - 1) * p, c % nct)
    return pl.pallas_call(
        functools.partial(_hy_conv_kernel, rb=rb, order=order),
        grid=(BC // nc, order, 2, nblk),
        in_specs=[col, col, col,
                  pl.BlockSpec((L, L), lambda c, o, p, k: (0, 0), pipeline_mode=pl.Buffered(1)),
                  pl.BlockSpec((2, L, L), lambda c, o, p, k: (0, 0, 0), pipeline_mode=pl.Buffered(1)),
                  pl.BlockSpec((1, rb, nc), kmap),
                  pl.BlockSpec((1, rb, nc), kmap),
                  pl.BlockSpec((1, 1, nc), lambda c, o, p, k: (o, 0, c % nct)),
                  pl.BlockSpec((1, 1, nc), lambda c, o, p, k: (o, 0, c % nct))],
        out_specs=col,
        out_shape=jax.ShapeDtypeStruct((L, BC), F32),
        scratch_shapes=[pltpu.VMEM((L, nc), F32), pltpu.VMEM((L, nc), BF16),
                        pltpu.VMEM((L, nc), BF16), pltpu.VMEM((L, nc), BF16)],
        compiler_params=_cparams(("arbitrary",) * 4),
        name="hyena_conv",
    )(v, x1, x2, cb, s12, kr, ki, kn, d_hy.reshape(order, 1, hy_w))


def _merge_kernel(y_ref, z_ref, x_ref, mod_ref, wg_ref, bg_ref, g5_ref, gh_ref, wo5_ref, woh_ref, g2_ref,
                  x1_ref, h2t_ref):
    y = y_ref[0].astype(F32) + y_ref[1].astype(F32)
    gy = _gelu(y)
    gate = jnp.dot(gy.astype(BF16), wg_ref[...], preferred_element_type=F32) + bg_ref[...]
    s5 = gy * _sigmoid(gate)
    n5 = _rms(s5, g5_ref[...])
    nh = _rms(z_ref[...], gh_ref[...])
    mix = (jnp.dot(n5.astype(BF16), wo5_ref[...], preferred_element_type=F32)
           + jnp.dot(nh.astype(BF16), woh_ref[...], preferred_element_type=F32))
    x1 = x_ref[0] + mod_ref[0, 2:3, :] * mix
    x1_ref[0] = x1
    h2 = _rms(x1, g2_ref[...]) * (1.0 + mod_ref[0, 4:5, :]) + mod_ref[0, 3:4, :]
    h2t_ref[0] = h2.T.astype(BF16)


def _merge(y, z, x, mod3, w_glu, b_glu, g5, gh, w_out, g2):
    B, L, D = x.shape
    W5 = w_glu.shape[0]
    WH = z.shape[1] // B
    tl = min(TOKEN_ROWS, L)
    full = lambda shp: pl.BlockSpec(shp, lambda b_, t: (0,) * len(shp))
    return pl.pallas_call(
        _merge_kernel,
        grid=(B, L // tl),
        in_specs=[pl.BlockSpec((2, tl, W5), lambda b_, t: (0, t, b_)),
                  pl.BlockSpec((tl, WH), lambda b_, t: (t, b_)),
                  pl.BlockSpec((1, tl, D), lambda b_, t: (b_, t, 0)),
                  pl.BlockSpec((1, N_MOD, D), lambda b_, t: (b_, 0, 0)),
                  full((W5, W5)), full((1, W5)), full((1, W5)), full((1, WH)),
                  full((W5, D)), full((WH, D)), full((1, D))],
        out_specs=[pl.BlockSpec((1, tl, D), lambda b_, t: (b_, t, 0)),
                   pl.BlockSpec((1, D, tl), lambda b_, t: (b_, 0, t))],
        out_shape=[jax.ShapeDtypeStruct((B, L, D), F32), jax.ShapeDtypeStruct((B, D, L), BF16)],
        compiler_params=_cparams(("arbitrary", "arbitrary")),
        name="merge_heads",
    )(y, z, x, mod3, w_glu.astype(BF16), b_glu.reshape(1, W5), g5.reshape(1, W5), gh.reshape(1, WH),
      w_out[:W5].astype(BF16), w_out[W5:].astype(BF16), g2.reshape(1, D))


def _topk_rank(s, k):
    vals = []
    cur = s
    rank = jnp.full(s.shape, float(k), F32)
    for i in range(k):
        m = jnp.max(cur, axis=0, keepdims=True)
        hit = cur == m
        vals.append(m)
        cur = jnp.where(hit, -jnp.inf, cur)
        rank = jnp.where(hit, float(i), rank)
    return jnp.concatenate(vals, axis=0), rank


def _topk_desc(s, k):
    vals = []
    cur = s
    for _ in range(k):
        m = jnp.max(cur, axis=0, keepdims=True)
        vals.append(m)
        cur = jnp.where(cur == m, -jnp.inf, cur)
    return jnp.concatenate(vals, axis=0)


def _gelu_packed(x):
    c = math.sqrt(2.0 / math.pi)
    inner = x * (c + (c * 0.044715) * (x * x))
    hx = 0.5 * x
    return hx + hx * jnp.tanh(inner)


PACK = 16
LANES = 128
SUBLANES = 8


def _packed_row(ref, lead, row, n_tiles):
    r8 = jnp.concatenate([ref[lead + (c, pl.ds(row, SUBLANES, stride=0), slice(None))]
                          for c in range(n_tiles)], axis=1)
    return jnp.concatenate([r8, r8], axis=0).astype(BF16)


def _peer_kernel(h_ref, x1_ref, mod_ref, wq_ref, k1_ref, k2_ref, u_ref, vt_ref, gf_ref, o_ref,
                 ea, cs, eb, r2b, acc, a0, a1, g0, g1, *, nb1, ne, topk):
    f = pl.program_id(0)
    total = pl.num_programs(0)
    H, NK, _ = k1_ref.shape
    T = h_ref.shape[2]
    f2 = f
    f3 = f
    par2 = (f2 // ne) % 2
    par3 = (f3 // ne) % 2
    e_mid = f2 % ne

    @pl.when(f == 0)
    def _():
        for buf in (a0, a1, g0, g1):
            buf[...] = jnp.zeros_like(buf)

    @pl.when((f % ne == 0) & (f < total))
    def _():
        par = (f // ne) % 2
        acc[par] = jnp.zeros(acc.shape[1:], F32)
        qt = jnp.dot(wq_ref[...], h_ref[0], preferred_element_type=F32)
        dk = k1_ref.shape[2]
        for hh in range(H):
            q1 = qt[(2 * hh) * dk:(2 * hh + 1) * dk, :]
            q2 = qt[(2 * hh + 1) * dk:(2 * hh + 2) * dk, :]
            s1f = jnp.dot(k1_ref[hh], q1.astype(BF16), preferred_element_type=F32)
            s2f = jnp.dot(k2_ref[hh], q2.astype(BF16), preferred_element_type=F32)
            for c0 in range(0, T, LANES):
                lc = slice(c0, c0 + LANES)
                s1, s2 = s1f[:, lc], s2f[:, lc]
                v1 = _topk_desc(s1, topk)
                v2, r2 = _topk_rank(s2, topk)
                cands = [v1[a:a + 1] + v2[b:b + 1]
                         for a in range(topk) for b in range(topk) if (a + 1) * (b + 1) <= topk]
                sel = _topk_desc(jnp.concatenate(cands, axis=0), topk)
                tau = sel[topk - 1:topk]
                mx = sel[0:1]
                rz = 1.0 / jnp.sum(jnp.exp(sel - mx), axis=0, keepdims=True)
                cnt = jnp.zeros((NK, LANES), F32)
                for a_ in range(topk):
                    c_a = jnp.sum(jnp.where(v1[a_:a_ + 1] + v2 >= tau, 1.0, 0.0), axis=0, keepdims=True)
                    cnt = jnp.where(s1 == v1[a_:a_ + 1], c_a, cnt)
                ea[par, hh, c0 // LANES] = jnp.exp(s1 - v1[0:1]) * rz
                cs[par, hh, c0 // LANES] = cnt
                eb[par, hh, :, lc] = jnp.exp(s2 - v2[0:1]).astype(BF16)
                r2b[par, hh, :, lc] = r2.astype(BF16)

    def stages(a_new, a_old, g_new, g_old):
        th = T // 2
        neb = nb1 * NK
        pieces = []
        for half in range(2):
            tc = slice(half * th, (half + 1) * th)
            for m0 in range(half * neb // 2, (half + 1) * neb // 2, MXU_ROWS):
                pieces.append(("a", slice(m0, m0 + MXU_ROWS), slice(None)))
            for m0 in range(0, vt_ref.shape[0], MXU_ROWS):
                pieces.append(("acc", slice(m0, m0 + MXU_ROWS), tc))

        def mxu_piece(kind, ms, tc):
            if kind == "a":
                a_new[ms, tc] = jnp.dot(u_ref[ms, :], h_ref[0, :, tc], preferred_element_type=F32)
            else:
                acc[par3, ms, tc] += jnp.dot(vt_ref[ms, :], g_old[:, tc], preferred_element_type=F32)

        def vpu_unit(j, r_lo, r_hi):
            i1 = e_mid * nb1 + j
            rss = [slice(r * PACK, (r + 1) * PACK) for r in range(r_lo, r_hi)]
            ws = [None] * len(rss)
            for hh in range(H):
                ea_p = _packed_row(ea, (par2, hh), i1, T // LANES)
                cs_p = _packed_row(cs, (par2, hh), i1, T // LANES)
                for k, rs in enumerate(rss):
                    term = jnp.where(r2b[par2, hh, rs, :] < cs_p, ea_p * eb[par2, hh, rs, :],
                                     jnp.zeros((), BF16))
                    ws[k] = term if hh == 0 else ws[k] + term
            for k, rs in enumerate(rss):
                rows = slice(j * NK + rs.start, j * NK + rs.stop)
                g_new[rows, :] = _gelu_packed(a_old[rows, :].astype(BF16)) * ws[k]

        nr = NK // PACK
        units = [(j, r0, r0 + VPU_UNIT) for j in range(nb1) for r0 in range(0, nr, VPU_UNIT)]
        weight = [u_ref.shape[1] if kind == "a" else neb for kind, _, _ in pieces]
        for piece in pieces:
            if piece[0] == "a":
                mxu_piece(*piece)
        for unit in units:
            vpu_unit(*unit)
        for piece in pieces:
            if piece[0] == "acc":
                mxu_piece(*piece)

    stages(a0, a0, g0, g0)

    @pl.when(f % ne == ne - 1)
    def _():
        x2 = x1_ref[0] + mod_ref[0, 5:6, :] * acc[par3].T
        o_ref[0] = _rms(x2, gf_ref[...])


def _peer(h2t, x1, mod3, wq, k1, k2, u_tab, v_tab, g_final):
    B, L, D = x1.shape
    H, NK, dk = k1.shape
    T = min(PEER_TOKENS, L)
    nb1 = PEER_BLOCKS
    nlt = L // T
    ne = NK // nb1
    total = B * nlt * ne
    t_in = lambda f: jnp.minimum(f // ne, B * nlt - 1)
    t_out = lambda f: f // ne
    once = lambda shp: pl.BlockSpec(shp, lambda f: (0,) * len(shp), pipeline_mode=pl.Buffered(1))
    tok = pl.BlockSpec((1, T, D), lambda f: (t_out(f) // nlt, t_out(f) % nlt, 0))
    return pl.pallas_call(
        functools.partial(_peer_kernel, nb1=nb1, ne=ne, topk=PEER_TOPK),
        grid=(total,),
        in_specs=[pl.BlockSpec((1, D, T), lambda f: (t_in(f) // nlt, 0, t_in(f) % nlt)), tok,
                  pl.BlockSpec((1, N_MOD, D), lambda f: (t_out(f) // nlt, 0, 0)),
                  once(wq.shape[::-1]), once(k1.shape), once(k2.shape),
                  pl.BlockSpec((nb1 * NK, D), lambda f: (jnp.minimum(f, total - 1) % ne, 0)),
                  pl.BlockSpec((D, nb1 * NK), lambda f: (0, f % ne)),
                  once((1, D))],
        out_specs=tok,
        out_shape=jax.ShapeDtypeStruct((B, L, D), F32),
        scratch_shapes=[pltpu.VMEM((2, H, T // LANES, NK, LANES), F32),
                        pltpu.VMEM((2, H, T // LANES, NK, LANES), F32),
                        pltpu.VMEM((2, H, NK, T), BF16), pltpu.VMEM((2, H, NK, T), BF16),
                        pltpu.VMEM((2, D, T), F32),
                        pltpu.VMEM((nb1 * NK, T), F32), pltpu.VMEM((nb1 * NK, T), F32),
                        pltpu.VMEM((nb1 * NK, T), BF16), pltpu.VMEM((nb1 * NK, T), BF16)],
        compiler_params=_cparams(("arbitrary",)),
        name="peer_dense",
    )(h2t, x1, mod3, wq.astype(BF16).T, k1.astype(BF16), k2.astype(BF16), u_tab.astype(BF16),
      v_tab.astype(BF16).T, g_final.reshape(1, D))


def kernel(x, c, ctx, c_ctx, w_ada, b_ada, g_norm1, g_norm2, w_in, b_in, s5_a_re, s5_a_im, s5_log_step, s5_b_re, s5_b_im, s5_c_re, s5_c_im, s5_d, w_glu, b_glu, hy_conv_w, hy_conv_b, hf_w1, hf_b1, hf_wh, hf_bh, hf_freq, hf_wout, hf_decay, hy_d, g_out_s5, g_out_hy, w_out, peer_wq, peer_k1, peer_k2, peer_u, peer_v, g_final):
    B, L, D = x.shape
    Lc = ctx.shape[1]
    depth = w_ada.shape[0]
    assert depth == 1, "single-layer block"
    l = 0
    S5W = w_glu.shape[1]
    HYW = g_out_hy.shape[1]
    order = hy_d.shape[1]
    _, _, G, P, Hs = s5_b_re.shape

    rpad = (-(B + 1)) % 8
    cc = jnp.concatenate([c, c_ctx[None], jnp.zeros((rpad, D), F32)], axis=0)
    mod = _ada(cc, w_ada[l], b_ada[l])
    mod_x = mod[:B].reshape(B, N_MOD, D)
    mod_c = mod[B:B + 1].reshape(1, N_MOD, D)

    w_in_bf = w_in[l].astype(BF16)
    u_x, v, xg1, xg2 = _inproj(x, mod_x, g_norm1[l], w_in_bf, b_in[l], hy_conv_w[l], hy_conv_b[l],
                               s5_w=S5W, hy_w=HYW, hyena=True)
    (u_c,) = _inproj(ctx, mod_c, g_norm1[l], w_in_bf[:, :S5W], b_in[l][:S5W], None, None,
                     s5_w=S5W, hy_w=HYW, hyena=False)

    lr, li, bbr, bbi = _s5_prep(s5_a_re[l], s5_a_im[l], s5_log_step[l], s5_b_re[l], s5_b_im[l])
    gin = MXU_W // Hs
    lam = jnp.stack([lr.reshape(2, G * P), li.reshape(2, G * P)], axis=1)
    tr = lambda m: jnp.swapaxes(m, 2, 3)
    wb = jnp.stack([_block_diag(tr(bbr), gin), _block_diag(tr(bbi), gin)], axis=1).astype(BF16)
    cm = jnp.stack([_block_diag(tr(s5_c_re[l]), gin), _block_diag(-tr(s5_c_im[l]), gin)], axis=1).astype(BF16)
    h_zero = jnp.zeros((2, 2, B, G * P), F32)
    _, h_ctx = _s5_scan(u_c.reshape(Lc, B, S5W), h_zero, lam, wb, cm, s5_d[l], emit_y=False)
    y5, _ = _s5_scan(u_x.reshape(L, B, S5W), h_ctx, lam, wb, cm, s5_d[l], emit_y=True)

    hs, hd = _hy_filters(L, hf_w1[l], hf_b1[l], hf_wh[l], hf_bh[l], hf_freq[l], hf_wout[l], hf_decay[l],
                         HYW, order)
    c32, s32, cb, s1b, s2b = _dft_mats(L)
    kr, ki, kn = _hy_spectra(c32, s32, hs, hd, HYW, order)
    z_hy = _hy_conv(v, xg1, xg2, cb, jnp.stack([s1b, s2b]), kr, ki, kn, hy_d[l], HYW)

    x1, h2 = _merge(y5.reshape(2, L, B * S5W), z_hy, x, mod_x, w_glu[l], b_glu[l],
                    g_out_s5[l], g_out_hy[l], w_out[l], g_norm2[l])
    return _peer(h2, x1, mod_x, peer_wq[l], peer_k1[l], peer_k2[l], peer_u[l], peer_v[l], g_final)
```

```python
import functools
import math

import jax
import jax.numpy as jnp
from jax import lax
from jax.experimental import pallas as pl
from jax.experimental.pallas import tpu as pltpu

F32 = jnp.float32
BF16 = jnp.bfloat16
HIGHEST = lax.Precision.HIGHEST

EPS = 1e-6
GRID_W = 64
PEER_TOPK = 16
N_MOD = 6

MXU_W = 256
VMEM_LIMIT = 56 * 1024 * 1024

ADA_COLS = 1024
TOKEN_ROWS = 512
S5_STEPS = 32
S5_COLS = 256
DFT_ROWS = 256
HY_COLS = 256
HY_ROWS = 1024
PEER_TOKENS = 512
PEER_BLOCKS = 8
MXU_ROWS = 512
VPU_UNIT = 8


def _cparams(sem):
    return pltpu.CompilerParams(dimension_semantics=sem, vmem_limit_bytes=VMEM_LIMIT)


def _gelu(x):
    c = math.sqrt(2.0 / math.pi)
    return 0.5 * x * (1.0 + jnp.tanh(c * (x + 0.044715 * (x * x * x))))


def _sigmoid(x):
    return 1.0 / (1.0 + jnp.exp(-x))


def _rms(x, g):
    return x * lax.rsqrt(jnp.mean(x * x, axis=-1, keepdims=True) + EPS) * g


def _ada_kernel(c_ref, w_ref, b_ref, o_ref):
    c = c_ref[...]
    s = c * _sigmoid(c)
    o_ref[...] = jnp.dot(s, w_ref[...], preferred_element_type=F32, precision=HIGHEST) + b_ref[...]


def _ada(cc, w, b):
    R, D = cc.shape
    N = w.shape[1]
    tn = ADA_COLS
    return pl.pallas_call(
        _ada_kernel,
        grid=(N // tn,),
        in_specs=[pl.BlockSpec((R, D), lambda j: (0, 0)),
                  pl.BlockSpec((D, tn), lambda j: (0, j)),
                  pl.BlockSpec((1, tn), lambda j: (0, j))],
        out_specs=pl.BlockSpec((R, tn), lambda j: (0, j)),
        out_shape=jax.ShapeDtypeStruct((R, N), F32),
        compiler_params=_cparams(("arbitrary",)),
        name="ada_mod",
    )(cc, w, b.reshape(1, N))


def _inproj_kernel(x_ref, mod_ref, g_ref, w_ref, b_ref, *rest, s5_w, hy_w, hyena):
    x = x_ref[0]
    h = _rms(x, g_ref[...])
    h = h * (1.0 + mod_ref[0, 1:2, :]) + mod_ref[0, 0:1, :]
    p = jnp.dot(h.astype(BF16), w_ref[...], preferred_element_type=F32) + b_ref[...]
    if not hyena:
        (u_ref,) = rest
        u_ref[...] = p
        return
    cw_ref, cb_ref, u_ref, v_ref, x1_ref, x2_ref = rest
    u_ref[...] = p[:, :s5_w]
    ph = p[:, s5_w:]
    tl = ph.shape[0]
    pos = lax.broadcasted_iota(jnp.int32, ph.shape, 0) % GRID_W
    prev = jnp.where(pos == 0, 0.0, pltpu.roll(ph, 1, axis=0))
    nxt = jnp.where(pos == GRID_W - 1, 0.0, pltpu.roll(ph, tl - 1, axis=0))
    q = cb_ref[...] + cw_ref[0:1, :] * prev + cw_ref[1:2, :] * ph + cw_ref[2:3, :] * nxt
    v_ref[...] = q[:, :hy_w]
    x1_ref[...] = q[:, hy_w:2 * hy_w]
    x2_ref[...] = q[:, 2 * hy_w:]


def _inproj(x, mod3, g, w_bf, b, conv_w, conv_b, *, s5_w, hy_w, hyena):
    B, L, D = x.shape
    N = w_bf.shape[1]
    tl = min(TOKEN_ROWS, L)
    nb = mod3.shape[0]
    mod_map = (lambda b_, t: (b_, 0, 0)) if nb > 1 else (lambda b_, t: (0, 0, 0))
    in_specs = [pl.BlockSpec((1, tl, D), lambda b_, t: (b_, t, 0)),
                pl.BlockSpec((1, N_MOD, D), mod_map),
                pl.BlockSpec((1, D), lambda b_, t: (0, 0)),
                pl.BlockSpec((D, N), lambda b_, t: (0, 0)),
                pl.BlockSpec((1, N), lambda b_, t: (0, 0))]
    args = [x, mod3, g.reshape(1, D), w_bf, b.reshape(1, N)]
    out_specs = [pl.BlockSpec((tl, s5_w), lambda b_, t: (t, b_))]
    out_shape = [jax.ShapeDtypeStruct((L, B * s5_w), F32)]
    if hyena:
        nh = N - s5_w
        in_specs += [pl.BlockSpec((3, nh), lambda b_, t: (0, 0)),
                     pl.BlockSpec((1, nh), lambda b_, t: (0, 0))]
        args += [conv_w, conv_b.reshape(1, nh)]
        out_specs += [pl.BlockSpec((tl, hy_w), lambda b_, t: (t, b_))] * 3
        out_shape += [jax.ShapeDtypeStruct((L, B * hy_w), F32)] * 3
    return pl.pallas_call(
        functools.partial(_inproj_kernel, s5_w=s5_w, hy_w=hy_w, hyena=hyena),
        grid=(B, L // tl),
        in_specs=in_specs, out_specs=out_specs, out_shape=out_shape,
        compiler_params=_cparams(("arbitrary", "arbitrary")),
        name="inproj_hy" if hyena else "inproj_ctx",
    )(*args)


def _s5_prep_kernel(are_ref, aim_ref, ls_ref, bre_ref, bim_ref, lr_ref, li_ref, bbr_ref, bbi_ref):
    a_re, a_im = are_ref[...], aim_ref[...]
    step = jnp.exp(ls_ref[...])
    er = jnp.exp(a_re * step)
    lr = er * jnp.cos(a_im * step)
    li = er * jnp.sin(a_im * step)
    nr, ni = lr - 1.0, li
    d2 = a_re * a_re + a_im * a_im
    kr = (nr * a_re + ni * a_im) / d2
    ki = (ni * a_re - nr * a_im) / d2
    br, bi = bre_ref[...], bim_ref[...]
    lr_ref[...] = lr
    li_ref[...] = li
    bbr_ref[...] = kr * br - ki * bi
    bbi_ref[...] = kr * bi + ki * br


def _s5_prep(a_re, a_im, log_step, b_re, b_im):
    two, G, P, H = b_re.shape
    n = two * G * P
    col = lambda a: a.reshape(n, 1)
    ls = jnp.broadcast_to(log_step[:, :, None], (two, G, P))
    outs = pl.pallas_call(
        _s5_prep_kernel,
        out_shape=[jax.ShapeDtypeStruct((n, 1), F32)] * 2 + [jax.ShapeDtypeStruct((n, H), F32)] * 2,
        name="s5_prep",
    )(col(a_re), col(a_im), col(ls), b_re.reshape(n, H), b_im.reshape(n, H))
    lr, li, bbr, bbi = outs
    return (lr.reshape(two, G, P), li.reshape(two, G, P),
            bbr.reshape(two, G, P, H), bbi.reshape(two, G, P, H))


def _block_diag(m, gb):
    two, G, R, C = m.shape
    m = m.reshape(two, G // gb, gb, R, C)
    eye = jnp.eye(gb, dtype=m.dtype)
    out = m[:, :, :, :, None, :] * eye[None, None, :, None, :, None]
    return out.reshape(two, G // gb, gb * R, gb * C)


def _s5_kernel(u_ref, h0_ref, lam_ref, wb_ref, cm_ref, d_ref, *rest, tl, cw, emit_y):
    if emit_y:
        y_ref, hfin_ref, sre, sim, hre, him = rest
    else:
        hfin_ref, sre, sim, hre, him = rest
    dr = pl.program_id(0)
    i = pl.program_id(1)
    B = u_ref.shape[1]
    SW = sre.shape[1]
    nsb = wb_ref.shape[2]
    kin = wb_ref.shape[3]
    kst = wb_ref.shape[4]

    @pl.when(i == 0)
    def _():
        hre[...] = h0_ref[0, 0]
        him[...] = h0_ref[0, 1]

    u = u_ref[...].reshape(tl * B, u_ref.shape[2])
    ub = u.astype(BF16)
    for j in range(nsb):
        uj = ub[:, j * kin:(j + 1) * kin]
        sre[:, j * kst:(j + 1) * kst] = jnp.dot(uj, wb_ref[0, 0, j], preferred_element_type=F32)
        sim[:, j * kst:(j + 1) * kst] = jnp.dot(uj, wb_ref[0, 1, j], preferred_element_type=F32)

    for c in range(SW // cw):
        cols = slice(c * cw, (c + 1) * cw)
        lr = jnp.broadcast_to(lam_ref[0, 0:1, cols], (B, cw))
        li = jnp.broadcast_to(lam_ref[0, 1:2, cols], (B, cw))

        def body(s, carry, cols=cols, lr=lr, li=li):
            h_r, h_i = carry
            t = jnp.where(dr == 0, s, tl - 1 - s)
            row = pl.multiple_of(t * B, B)
            n_r = lr * h_r - li * h_i + sre[pl.ds(row, B), cols]
            n_i = lr * h_i + li * h_r + sim[pl.ds(row, B), cols]
            sre[pl.ds(row, B), cols] = n_r
            sim[pl.ds(row, B), cols] = n_i
            return n_r, n_i

        h_r, h_i = lax.fori_loop(0, tl, body, (hre[:, cols], him[:, cols]), unroll=2)
        hre[:, cols] = h_r
        him[:, cols] = h_i

    hfin_ref[0, 0] = hre[...]
    hfin_ref[0, 1] = him[...]

    if emit_y:
        kout = cm_ref.shape[4]
        dsel = d_ref[...] * jnp.where(dr == 0, 1.0, 0.0)
        for n in range(nsb):
            y = (jnp.dot(sre[:, n * kst:(n + 1) * kst].astype(BF16), cm_ref[0, 0, n], preferred_element_type=F32)
                 + jnp.dot(sim[:, n * kst:(n + 1) * kst].astype(BF16), cm_ref[0, 1, n], preferred_element_type=F32))
            oc = slice(n * kout, (n + 1) * kout)
            y = y + dsel[:, oc] * u[:, oc]
            y_ref[0, :, :, oc] = y.reshape(tl, B, kout).astype(y_ref.dtype)


def _s5_scan(u3, h0, lam, wb, cm, d, *, emit_y):
    L, B, W = u3.shape
    SW = lam.shape[2]
    tl = min(S5_STEPS, L)
    nT = L // tl
    tmap = lambda dr, i: i + dr * (nT - 1 - 2 * i)
    in_specs = [pl.BlockSpec((tl, B, W), lambda dr, i: (tmap(dr, i), 0, 0)),
                pl.BlockSpec((1, 2, B, SW), lambda dr, i: (dr, 0, 0, 0)),
                pl.BlockSpec((1, 2, SW), lambda dr, i: (dr, 0, 0)),
                pl.BlockSpec((1,) + wb.shape[1:], lambda dr, i: (dr, 0, 0, 0, 0)),
                pl.BlockSpec((1,) + cm.shape[1:], lambda dr, i: (dr, 0, 0, 0, 0)),
                pl.BlockSpec((1, W), lambda dr, i: (0, 0))]
    out_specs = [pl.BlockSpec((1, 2, B, SW), lambda dr, i: (dr, 0, 0, 0))]
    out_shape = [jax.ShapeDtypeStruct((2, 2, B, SW), F32)]
    if emit_y:
        out_specs = [pl.BlockSpec((1, tl, B, W), lambda dr, i: (dr, tmap(dr, i), 0, 0))] + out_specs
        out_shape = [jax.ShapeDtypeStruct((2, L, B, W), BF16)] + out_shape
    res = pl.pallas_call(
        functools.partial(_s5_kernel, tl=tl, cw=S5_COLS, emit_y=emit_y),
        grid=(2, nT),
        in_specs=in_specs, out_specs=out_specs, out_shape=out_shape,
        scratch_shapes=[pltpu.VMEM((tl * B, SW), F32), pltpu.VMEM((tl * B, SW), F32),
                        pltpu.VMEM((B, SW), F32), pltpu.VMEM((B, SW), F32)],
        compiler_params=_cparams(("arbitrary", "arbitrary")),
        name="s5_scan_x" if emit_y else "s5_scan_ctx",
    )(u3, h0, lam, wb, cm, d.reshape(1, W))
    return res if emit_y else (None, res[0])


def _hy_filter_kernel(w1t_ref, w1c_ref, w1s_ref, b1_ref, wh_ref, bh_ref, fr_ref, wout_ref, dec_ref,
                      bands_ref, hs_ref, hd_ref, *, L, hy_w):
    pos = lax.broadcasted_iota(jnp.int32, (L, 1), 0).astype(F32)
    t = pos / max(L - 1, 1)
    ang = (2.0 * math.pi / L) * pos * bands_ref[...]
    fr = fr_ref[...]
    dot = functools.partial(jnp.dot, preferred_element_type=F32, precision=HIGHEST)
    pre = t * w1t_ref[...] + dot(jnp.cos(ang), w1c_ref[...]) + dot(-jnp.sin(ang), w1s_ref[...]) + b1_ref[...]
    hid = jnp.sin(fr * pre)
    for i in range(wh_ref.shape[0]):
        hid = jnp.sin(fr * (dot(hid, wh_ref[i]) + bh_ref[i]))
    h = dot(hid, wout_ref[...]) * jnp.exp(-t * jnp.abs(dec_ref[...]))
    nrm = jnp.sum(h * h, axis=0, keepdims=True)
    nrm = nrm[:, :hy_w] + nrm[:, hy_w:]
    scale = lax.rsqrt(nrm + EPS)
    hf = h[:, :hy_w] * scale
    hb = h[:, hy_w:] * scale
    hb0 = jnp.where(lax.broadcasted_iota(jnp.int32, hb.shape, 0) == 0, 0.0, hb)
    hs_ref[...] = hf + hb0
    hd_ref[...] = hf - hb0


def _hy_filters(L, w1, b1, wh, bh, freq, wout, decay, hy_w, order):
    nb = (w1.shape[0] - 1) // 2
    hid = w1.shape[1]
    bands = jnp.linspace(1e-4, nb - 1, nb, dtype=F32).reshape(1, nb)
    full = lambda shp: pl.BlockSpec(shp, lambda o: (0,) * len(shp))
    return pl.pallas_call(
        functools.partial(_hy_filter_kernel, L=L, hy_w=hy_w),
        grid=(order,),
        in_specs=[full((1, hid)), full((nb, hid)), full((nb, hid)), full((1, hid)),
                  full(wh.shape), full((wh.shape[0], 1, hid)), full((1, hid)),
                  pl.BlockSpec((hid, 2 * hy_w), lambda o: (0, o)),
                  pl.BlockSpec((1, 2 * hy_w), lambda o: (0, o)),
                  full((1, nb))],
        out_specs=[pl.BlockSpec((L, hy_w), lambda o: (0, o))] * 2,
        out_shape=[jax.ShapeDtypeStruct((L, order * hy_w), F32)] * 2,
        compiler_params=_cparams(("arbitrary",)),
        name="hyena_filters",
    )(w1[0:1], w1[1:1 + nb], w1[1 + nb:], b1.reshape(1, hid), wh, bh.reshape(wh.shape[0], 1, hid),
      freq.reshape(1, hid), wout, decay.reshape(1, -1), bands)


def _dft_kernel(c32_ref, s32_ref, cb_ref, s1_ref, s2_ref, *, L, rb):
    i = pl.program_id(0)
    f = lax.broadcasted_iota(jnp.int32, (rb, L), 0) + i * rb
    t = lax.broadcasted_iota(jnp.int32, (rb, L), 1)
    k = (f * t) % (2 * L)
    ang = k.astype(F32) * (math.pi / L)
    c = jnp.cos(ang)
    s = -jnp.sin(ang)
    c32_ref[...] = c
    s32_ref[...] = s
    cb_ref[...] = c.astype(BF16)
    sign_t = jnp.where(t % 2 == 0, 1.0, -1.0)
    sign_f = jnp.where(f % 2 == 0, 1.0, -1.0)
    s1_ref[...] = jnp.where(f == 0, sign_t, s).astype(BF16)
    s2_ref[...] = jnp.where(t == 0, sign_f, s).astype(BF16)


def _dft_mats(L):
    rb = min(DFT_ROWS, L)
    spec = pl.BlockSpec((rb, L), lambda i: (i, 0))
    return pl.pallas_call(
        functools.partial(_dft_kernel, L=L, rb=rb),
        grid=(L // rb,),
        in_specs=[],
        out_specs=[spec] * 5,
        out_shape=[jax.ShapeDtypeStruct((L, L), F32)] * 2 + [jax.ShapeDtypeStruct((L, L), BF16)] * 3,
        compiler_params=_cparams(("arbitrary",)),
        name="dft_mats",
    )()


def _spec_kernel(c_ref, s_ref, hs_ref, hd_ref, kr_ref, ki_ref, kn_ref, *, L, rb):
    i = pl.program_id(1)
    f = lax.broadcasted_iota(jnp.int32, (rb, 1), 0) + i * rb
    w = jnp.where(f == 0, 1.0, 2.0) * (1.0 / (2 * L))
    dot = functools.partial(jnp.dot, preferred_element_type=F32, precision=HIGHEST)
    hs = hs_ref[...]
    kr_ref[0] = w * dot(c_ref[...], hs)
    ki_ref[0] = w * dot(s_ref[...], hd_ref[...])
    sign = jnp.where(lax.broadcasted_iota(jnp.int32, (L, 1), 0) % 2 == 0, 1.0, -1.0)
    kn_ref[0] = jnp.sum(sign * hs, axis=0, keepdims=True) * (1.0 / (2 * L))


def _hy_spectra(c32, s32, hs, hd, hy_w, order):
    L = c32.shape[0]
    rb = min(DFT_ROWS, L)
    mat = pl.BlockSpec((rb, L), lambda o, i: (i, 0))
    hsp = pl.BlockSpec((L, hy_w), lambda o, i: (0, o))
    return pl.pallas_call(
        functools.partial(_spec_kernel, L=L, rb=rb),
        grid=(order, L // rb),
        in_specs=[mat, mat, hsp, hsp],
        out_specs=[pl.BlockSpec((1, rb, hy_w), lambda o, i: (o, i, 0))] * 2
                  + [pl.BlockSpec((1, 1, hy_w), lambda o, i: (o, 0, 0))],
        out_shape=[jax.ShapeDtypeStruct((order, L, hy_w), F32)] * 2
                  + [jax.ShapeDtypeStruct((order, 1, hy_w), F32)],
        compiler_params=_cparams(("arbitrary", "arbitrary")),
        name="hyena_spectra",
    )(c32, s32, hs, hd)


def _hy_conv_kernel(v_ref, x1_ref, x2_ref, c_ref, s_ref, kr_ref, ki_ref, kn_ref, d_ref, o_ref,
                    zf, zb, pre, pim, *, rb, order):
    o = pl.program_id(1)
    ph = pl.program_id(2)
    blk = pl.program_id(3)
    rows = pl.ds(pl.multiple_of(blk * rb, rb), rb)

    @pl.when((o == 0) & (ph == 0) & (blk == 0))
    def _():
        v = v_ref[...]
        zf[...] = v
        zb[...] = v.astype(BF16)

    @pl.when(ph == 0)
    def _():
        z = zb[...]
        zr = jnp.dot(c_ref[rows, :], z, preferred_element_type=F32)
        zi = jnp.dot(s_ref[0, rows, :], z, preferred_element_type=F32)
        kr, ki = kr_ref[0], ki_ref[0]
        first = (lax.broadcasted_iota(jnp.int32, zr.shape, 0) == 0) & (blk == 0)
        pr = kr * zr - jnp.where(first, 0.0, ki * zi)
        pi = jnp.where(first, kn_ref[0] * zi, kr * zi + ki * zr)
        pre[rows, :] = pr.astype(BF16)
        pim[rows, :] = pi.astype(BF16)

    @pl.when(ph == 1)
    def _():
        y = (jnp.dot(c_ref[rows, :], pre[...], preferred_element_type=F32)
             + jnp.dot(s_ref[1, rows, :], pim[...], preferred_element_type=F32))
        zin = zf[rows, :]
        gate = jnp.where(o == 0, x1_ref[rows, :], x2_ref[rows, :])
        zn = gate * (y + d_ref[0] * zin)
        zf[rows, :] = zn
        zb[rows, :] = zn.astype(BF16)

        @pl.when(o == order - 1)
        def _():
            o_ref[rows, :] = zn


def _hy_conv(v, x1, x2, cb, s12, kr, ki, kn, d_hy, hy_w):
    L, BC = v.shape
    order = kr.shape[0]
    nc = HY_COLS
    rb = min(HY_ROWS, L)
    nblk = L // rb
    nct = hy_w // nc
    col = pl.BlockSpec((L, nc), lambda c, o, p, k: (0, c))
    kmap = lambda c, o, p, k: (o, k * (1 - p) + (nblk - 1) * p, c % nct)
    return pl.pallas_call(
        functools.partial(_hy_conv_kernel, rb=rb, order=order),
        grid=(BC // nc, order, 2, nblk),
        in_specs=[col, col, col,
                  pl.BlockSpec((L, L), lambda c, o, p, k: (0, 0), pipeline_mode=pl.Buffered(1)),
                  pl.BlockSpec((2, L, L), lambda c, o, p, k: (0, 0, 0), pipeline_mode=pl.Buffered(1)),
                  pl.BlockSpec((1, rb, nc), kmap),
                  pl.BlockSpec((1, rb, nc), kmap),
                  pl.BlockSpec((1, 1, nc), lambda c, o, p, k: (o, 0, c % nct)),
                  pl.BlockSpec((1, 1, nc), lambda c, o, p, k: (o, 0, c % nct))],
        out_specs=col,
        out_shape=jax.ShapeDtypeStruct((L, BC), F32),
        scratch_shapes=[pltpu.VMEM((L, nc), F32), pltpu.VMEM((L, nc), BF16),
                        pltpu.VMEM((L, nc), BF16), pltpu.VMEM((L, nc), BF16)],
        compiler_params=_cparams(("arbitrary",) * 4),
        name="hyena_conv",
    )(v, x1, x2, cb, s12, kr, ki, kn, d_hy.reshape(order, 1, hy_w))


def _merge_kernel(y_ref, z_ref, x_ref, mod_ref, wg_ref, bg_ref, g5_ref, gh_ref, wo5_ref, woh_ref, g2_ref,
                  x1_ref, h2t_ref):
    y = y_ref[0].astype(F32) + y_ref[1].astype(F32)
    gy = _gelu(y)
    gate = jnp.dot(gy.astype(BF16), wg_ref[...], preferred_element_type=F32) + bg_ref[...]
    s5 = gy * _sigmoid(gate)
    n5 = _rms(s5, g5_ref[...])
    nh = _rms(z_ref[...], gh_ref[...])
    mix = (jnp.dot(n5.astype(BF16), wo5_ref[...], preferred_element_type=F32)
           + jnp.dot(nh.astype(BF16), woh_ref[...], preferred_element_type=F32))
    x1 = x_ref[0] + mod_ref[0, 2:3, :] * mix
    x1_ref[0] = x1
    h2 = _rms(x1, g2_ref[...]) * (1.0 + mod_ref[0, 4:5, :]) + mod_ref[0, 3:4, :]
    h2t_ref[0] = h2.T.astype(BF16)


def _merge(y, z, x, mod3, w_glu, b_glu, g5, gh, w_out, g2):
    B, L, D = x.shape
    W5 = w_glu.shape[0]
    WH = z.shape[1] // B
    tl = min(TOKEN_ROWS, L)
    full = lambda shp: pl.BlockSpec(shp, lambda b_, t: (0,) * len(shp))
    return pl.pallas_call(
        _merge_kernel,
        grid=(B, L // tl),
        in_specs=[pl.BlockSpec((2, tl, W5), lambda b_, t: (0, t, b_)),
                  pl.BlockSpec((tl, WH), lambda b_, t: (t, b_)),
                  pl.BlockSpec((1, tl, D), lambda b_, t: (b_, t, 0)),
                  pl.BlockSpec((1, N_MOD, D), lambda b_, t: (b_, 0, 0)),
                  full((W5, W5)), full((1, W5)), full((1, W5)), full((1, WH)),
                  full((W5, D)), full((WH, D)), full((1, D))],
        out_specs=[pl.BlockSpec((1, tl, D), lambda b_, t: (b_, t, 0)),
                   pl.BlockSpec((1, D, tl), lambda b_, t: (b_, 0, t))],
        out_shape=[jax.ShapeDtypeStruct((B, L, D), F32), jax.ShapeDtypeStruct((B, D, L), BF16)],
        compiler_params=_cparams(("arbitrary", "arbitrary")),
        name="merge_heads",
    )(y, z, x, mod3, w_glu.astype(BF16), b_glu.reshape(1, W5), g5.reshape(1, W5), gh.reshape(1, WH),
      w_out[:W5].astype(BF16), w_out[W5:].astype(BF16), g2.reshape(1, D))


def _topk_rank(s, k):
    vals = []
    cur = s
    rank = jnp.full(s.shape, float(k), F32)
    for i in range(k):
        m = jnp.max(cur, axis=0, keepdims=True)
        hit = cur == m
        vals.append(m)
        cur = jnp.where(hit, -jnp.inf, cur)
        rank = jnp.where(hit, float(i), rank)
    return jnp.concatenate(vals, axis=0), rank


def _topk_desc(s, k):
    vals = []
    cur = s
    for _ in range(k):
        m = jnp.max(cur, axis=0, keepdims=True)
        vals.append(m)
        cur = jnp.where(cur == m, -jnp.inf, cur)
    return jnp.concatenate(vals, axis=0)


def _gelu_packed(x):
    c = math.sqrt(2.0 / math.pi)
    inner = x * (c + (c * 0.044715) * (x * x))
    hx = 0.5 * x
    return hx + hx * jnp.tanh(inner)


PACK = 16
LANES = 128
SUBLANES = 8


def _packed_row(ref, lead, row, n_tiles):
    r8 = jnp.concatenate([ref[lead + (c, pl.ds(row, SUBLANES, stride=0), slice(None))]
                          for c in range(n_tiles)], axis=1)
    return jnp.concatenate([r8, r8], axis=0).astype(BF16)


def _peer_kernel(h_ref, x1_ref, mod_ref, wq_ref, k1_ref, k2_ref, u_ref, vt_ref, gf_ref, o_ref,
                 ea, cs, eb, r2b, acc, a0, a1, g0, g1, *, nb1, ne, topk):
    f = pl.program_id(0)
    total = pl.num_programs(0)
    H, NK, _ = k1_ref.shape
    T = h_ref.shape[2]
    f2 = f
    f3 = f
    par2 = (f2 // ne) % 2
    par3 = (f3 // ne) % 2
    e_mid = f2 % ne

    @pl.when(f == 0)
    def _():
        for buf in (a0, a1, g0, g1):
            buf[...] = jnp.zeros_like(buf)

    @pl.when((f % ne == 0) & (f < total))
    def _():
        par = (f // ne) % 2
        acc[par] = jnp.zeros(acc.shape[1:], F32)
        qt = jnp.dot(wq_ref[...], h_ref[0], preferred_element_type=F32)
        dk = k1_ref.shape[2]
        for hh in range(H):
            q1 = qt[(2 * hh) * dk:(2 * hh + 1) * dk, :]
            q2 = qt[(2 * hh + 1) * dk:(2 * hh + 2) * dk, :]
            s1f = jnp.dot(k1_ref[hh], q1.astype(BF16), preferred_element_type=F32)
            s2f = jnp.dot(k2_ref[hh], q2.astype(BF16), preferred_element_type=F32)
            for c0 in range(0, T, LANES):
                lc = slice(c0, c0 + LANES)
                s1, s2 = s1f[:, lc], s2f[:, lc]
                v1 = _topk_desc(s1, topk)
                v2, r2 = _topk_rank(s2, topk)
                cands = [v1[a:a + 1] + v2[b:b + 1]
                         for a in range(topk) for b in range(topk) if (a + 1) * (b + 1) <= topk]
                sel = _topk_desc(jnp.concatenate(cands, axis=0), topk)
                tau = sel[topk - 1:topk]
                mx = sel[0:1]
                rz = 1.0 / jnp.sum(jnp.exp(sel - mx), axis=0, keepdims=True)
                cnt = jnp.zeros((NK, LANES), F32)
                for a_ in range(topk):
                    c_a = jnp.sum(jnp.where(v1[a_:a_ + 1] + v2 >= tau, 1.0, 0.0), axis=0, keepdims=True)
                    cnt = jnp.where(s1 == v1[a_:a_ + 1], c_a, cnt)
                ea[par, hh, c0 // LANES] = jnp.exp(s1 - v1[0:1]) * rz
                cs[par, hh, c0 // LANES] = cnt
                eb[par, hh, :, lc] = jnp.exp(s2 - v2[0:1]).astype(BF16)
                r2b[par, hh, :, lc] = r2.astype(BF16)

    def stages(a_new, a_old, g_new, g_old):
        th = T // 2
        neb = nb1 * NK
        pieces = []
        for half in range(2):
            tc = slice(half * th, (half + 1) * th)
            for m0 in range(half * neb // 2, (half + 1) * neb // 2, MXU_ROWS):
                pieces.append(("a", slice(m0, m0 + MXU_ROWS), slice(None)))
            for m0 in range(0, vt_ref.shape[0], MXU_ROWS):
                pieces.append(("acc", slice(m0, m0 + MXU_ROWS), tc))

        def mxu_piece(kind, ms, tc):
            if kind == "a":
                a_new[ms, tc] = jnp.dot(u_ref[ms, :], h_ref[0, :, tc],
                                        preferred_element_type=F32).astype(a_new.dtype)
            else:
                acc[par3, ms, tc] += jnp.dot(vt_ref[ms, :], g_old[:, tc], preferred_element_type=F32)

        def vpu_unit(j, r_lo, r_hi):
            i1 = e_mid * nb1 + j
            rss = [slice(r * PACK, (r + 1) * PACK) for r in range(r_lo, r_hi)]
            ws = [None] * len(rss)
            for hh in range(H):
                ea_p = _packed_row(ea, (par2, hh), i1, T // LANES)
                cs_p = _packed_row(cs, (par2, hh), i1, T // LANES)
                for k, rs in enumerate(rss):
                    term = jnp.where(r2b[par2, hh, rs, :] < cs_p, ea_p * eb[par2, hh, rs, :],
                                     jnp.zeros((), BF16))
                    ws[k] = term if hh == 0 else ws[k] + term
            for k, rs in enumerate(rss):
                rows = slice(j * NK + rs.start, j * NK + rs.stop)
                g_new[rows, :] = _gelu_packed(a_old[rows, :].astype(BF16)) * ws[k]

        nr = NK // PACK
        units = [(j, r0, r0 + VPU_UNIT) for j in range(nb1) for r0 in range(0, nr, VPU_UNIT)]
        weight = [u_ref.shape[1] if kind == "a" else neb for kind, _, _ in pieces]
        for piece in pieces:
            if piece[0] == "a":
                mxu_piece(*piece)
        for unit in units:
            vpu_unit(*unit)
        for piece in pieces:
            if piece[0] == "acc":
                mxu_piece(*piece)

    stages(a0, a0, g0, g0)

    @pl.when(f % ne == ne - 1)
    def _():
        x2 = x1_ref[0] + mod_ref[0, 5:6, :] * acc[par3].T
        o_ref[0] = _rms(x2, gf_ref[...])


def _peer(h2t, x1, mod3, wq, k1, k2, u_tab, v_tab, g_final):
    B, L, D = x1.shape
    H, NK, dk = k1.shape
    T = min(PEER_TOKENS, L)
    nb1 = PEER_BLOCKS
    nlt = L // T
    ne = NK // nb1
    total = B * nlt * ne
    t_in = lambda f: jnp.minimum(f // ne, B * nlt - 1)
    t_out = lambda f: f // ne
    once = lambda shp: pl.BlockSpec(shp, lambda f: (0,) * len(shp), pipeline_mode=pl.Buffered(1))
    tok = pl.BlockSpec((1, T, D), lambda f: (t_out(f) // nlt, t_out(f) % nlt, 0))
    return pl.pallas_call(
        functools.partial(_peer_kernel, nb1=nb1, ne=ne, topk=PEER_TOPK),
        grid=(total,),
        in_specs=[pl.BlockSpec((1, D, T), lambda f: (t_in(f) // nlt, 0, t_in(f) % nlt)), tok,
                  pl.BlockSpec((1, N_MOD, D), lambda f: (t_out(f) // nlt, 0, 0)),
                  once(wq.shape[::-1]), once(k1.shape), once(k2.shape),
                  pl.BlockSpec((nb1 * NK, D), lambda f: (jnp.minimum(f, total - 1) % ne, 0)),
                  pl.BlockSpec((D, nb1 * NK), lambda f: (0, f % ne)),
                  once((1, D))],
        out_specs=tok,
        out_shape=jax.ShapeDtypeStruct((B, L, D), F32),
        scratch_shapes=[pltpu.VMEM((2, H, T // LANES, NK, LANES), F32),
                        pltpu.VMEM((2, H, T // LANES, NK, LANES), F32),
                        pltpu.VMEM((2, H, NK, T), BF16), pltpu.VMEM((2, H, NK, T), BF16),
                        pltpu.VMEM((2, D, T), F32),
                        pltpu.VMEM((nb1 * NK, T), BF16), pltpu.VMEM((nb1 * NK, T), BF16),
                        pltpu.VMEM((nb1 * NK, T), BF16), pltpu.VMEM((nb1 * NK, T), BF16)],
        compiler_params=_cparams(("arbitrary",)),
        name="peer_dense",
    )(h2t, x1, mod3, wq.astype(BF16).T, k1.astype(BF16), k2.astype(BF16), u_tab.astype(BF16),
      v_tab.astype(BF16).T, g_final.reshape(1, D))


def kernel(x, c, ctx, c_ctx, w_ada, b_ada, g_norm1, g_norm2, w_in, b_in, s5_a_re, s5_a_im, s5_log_step, s5_b_re, s5_b_im, s5_c_re, s5_c_im, s5_d, w_glu, b_glu, hy_conv_w, hy_conv_b, hf_w1, hf_b1, hf_wh, hf_bh, hf_freq, hf_wout, hf_decay, hy_d, g_out_s5, g_out_hy, w_out, peer_wq, peer_k1, peer_k2, peer_u, peer_v, g_final):
    B, L, D = x.shape
    Lc = ctx.shape[1]
    depth = w_ada.shape[0]
    assert depth == 1, "single-layer block"
    l = 0
    S5W = w_glu.shape[1]
    HYW = g_out_hy.shape[1]
    order = hy_d.shape[1]
    _, _, G, P, Hs = s5_b_re.shape

    rpad = (-(B + 1)) % 8
    cc = jnp.concatenate([c, c_ctx[None], jnp.zeros((rpad, D), F32)], axis=0)
    mod = _ada(cc, w_ada[l], b_ada[l])
    mod_x = mod[:B].reshape(B, N_MOD, D)
    mod_c = mod[B:B + 1].reshape(1, N_MOD, D)

    w_in_bf = w_in[l].astype(BF16)
    u_x, v, xg1, xg2 = _inproj(x, mod_x, g_norm1[l], w_in_bf, b_in[l], hy_conv_w[l], hy_conv_b[l],
                               s5_w=S5W, hy_w=HYW, hyena=True)
    (u_c,) = _inproj(ctx, mod_c, g_norm1[l], w_in_bf[:, :S5W], b_in[l][:S5W], None, None,
                     s5_w=S5W, hy_w=HYW, hyena=False)

    lr, li, bbr, bbi = _s5_prep(s5_a_re[l], s5_a_im[l], s5_log_step[l], s5_b_re[l], s5_b_im[l])
    gin = MXU_W // Hs
    lam = jnp.stack([lr.reshape(2, G * P), li.reshape(2, G * P)], axis=1)
    tr = lambda m: jnp.swapaxes(m, 2, 3)
    wb = jnp.stack([_block_diag(tr(bbr), gin), _block_diag(tr(bbi), gin)], axis=1).astype(BF16)
    cm = jnp.stack([_block_diag(tr(s5_c_re[l]), gin), _block_diag(-tr(s5_c_im[l]), gin)], axis=1).astype(BF16)
    h_zero = jnp.zeros((2, 2, B, G * P), F32)
    _, h_ctx = _s5_scan(u_c.reshape(Lc, B, S5W), h_zero, lam, wb, cm, s5_d[l], emit_y=False)
    y5, _ = _s5_scan(u_x.reshape(L, B, S5W), h_ctx, lam, wb, cm, s5_d[l], emit_y=True)

    hs, hd = _hy_filters(L, hf_w1[l], hf_b1[l], hf_wh[l], hf_bh[l], hf_freq[l], hf_wout[l], hf_decay[l],
                         HYW, order)
    c32, s32, cb, s1b, s2b = _dft_mats(L)
    kr, ki, kn = _hy_spectra(c32, s32, hs, hd, HYW, order)
    z_hy = _hy_conv(v, xg1, xg2, cb, jnp.stack([s1b, s2b]), kr, ki, kn, hy_d[l], HYW)

    x1, h2 = _merge(y5.reshape(2, L, B * S5W), z_hy, x, mod_x, w_glu[l], b_glu[l],
                    g_out_s5[l], g_out_hy[l], w_out[l], g_norm2[l])
    return _peer(h2, x1, mod_x, peer_wq[l], peer_k1[l], peer_k2[l], peer_u[l], peer_v[l], g_final)
```
